```python
import jax, jax.numpy as jnp
from jax import lax
import numpy as np

D_MODEL = 1024
BATCH = 16
SEQ = 2048
DEPTH = 2

CHUNK = 64
Q_BLOCK = 128
N_MIXERS = 2
N_RET_LAYERS = (DEPTH + 1) // 2
N_MLA_LAYERS = DEPTH // 2
RMS_EPS = 1e-6
ROPE_THETA = 10000.0

RET_HEADS = D_MODEL // 256
RET_QK_DIM = 256
RET_V_DIM = 2 * D_MODEL // RET_HEADS
RET_GAMMA_BASE = -5.0

MLA_HEADS = D_MODEL // 128
MLA_Q_RANK = 384
MLA_KV_RANK = D_MODEL // 4
MLA_NOPE_DIM = 128
MLA_ROPE_DIM = 64
MLA_V_DIM = 128
MLA_QK_DIM = MLA_NOPE_DIM + MLA_ROPE_DIM
MASK_VALUE = -1e30

FFN_DIM = 2816
CONV_WIDTH = 3

kernel_name = "hybrid_retention_mla_convffn_trunk"


def rms_norm(x, gain):
    xf = x.astype(jnp.float32)
    y = xf * lax.rsqrt(jnp.mean(xf * xf, axis=-1, keepdims=True) + RMS_EPS)
    return (y * gain.astype(jnp.float32)).astype(x.dtype)


def rope(x, pos):
    half = x.shape[-1] // 2
    inv_freq = ROPE_THETA ** (-jnp.arange(half, dtype=jnp.float32) / half)
    ang = pos.astype(jnp.float32)[:, None] * inv_freq[None, :]
    cos = jnp.cos(ang)[None, :, None, :]
    sin = jnp.sin(ang)[None, :, None, :]
    xf = x.astype(jnp.float32)
    x1, x2 = xf[..., :half], xf[..., half:]
    return jnp.concatenate([x1 * cos - x2 * sin, x2 * cos + x1 * sin], axis=-1).astype(x.dtype)


def retention_mixer(h, w_in, gn_gain, w_out):
    B, S, _ = h.shape
    H, dk, dv = RET_HEADS, RET_QK_DIM, RET_V_DIM
    proj = h @ w_in
    q, k, v, g = jnp.split(proj, [H * dk, 2 * H * dk, 2 * H * dk + H * dv], axis=-1)
    pos = jnp.arange(S)
    q = rope(q.reshape(B, S, H, dk), pos)
    k = rope(k.reshape(B, S, H, dk), pos) * (dk ** -0.5)
    v = v.reshape(B, S, H, dv)
    n_chunks = S // CHUNK

    def to_chunks(t):
        return t.reshape(B, n_chunks, CHUNK, H, t.shape[-1]).transpose(1, 0, 3, 2, 4)

    log_gamma = jnp.log1p(-jnp.exp2(RET_GAMMA_BASE - jnp.arange(H, dtype=jnp.float32)))
    idx = jnp.arange(CHUNK, dtype=jnp.float32)
    intra_decay = jnp.exp(log_gamma[:, None, None] * jnp.abs(idx[:, None] - idx[None, :]))
    q_decay = jnp.exp(log_gamma[:, None] * (idx + 1.0))[:, :, None]
    k_decay = jnp.exp(log_gamma[:, None] * (CHUNK - 1.0 - idx))[:, :, None]
    chunk_decay = jnp.exp(log_gamma * CHUNK)[:, None, None]

    def step(state, qkv):
        qc, kc, vc = qkv
        scores = jnp.einsum('bhid,bhjd->bhij', qc, kc) * intra_decay
        inner = jnp.einsum('bhij,bhjv->bhiv', scores, vc)
        cross = jnp.einsum('bhid,bhdv->bhiv', qc * q_decay, state)
        state = state * chunk_decay + jnp.einsum('bhjd,bhjv->bhdv', kc * k_decay, vc)
        return state, inner + cross

    state0 = jnp.zeros((B, H, dk, dv), jnp.float32)
    _, out = lax.scan(step, state0, (to_chunks(q), to_chunks(k), to_chunks(v)))
    out = out.transpose(1, 0, 3, 2, 4).reshape(B, S, H, dv)
    out = rms_norm(out, gn_gain).astype(h.dtype)
    out = out.reshape(B, S, H * dv) * jax.nn.silu(g)
    return out @ w_out


def mla_mixer(h, w_in, q_norm_g, w_qb, kv_norm_g, w_kvb, q_head_g, k_head_g, w_out):
    B, S, _ = h.shape
    H = MLA_HEADS
    proj = h @ w_in
    c_q, c_kv, k_rope = jnp.split(proj, [MLA_Q_RANK, MLA_Q_RANK + MLA_KV_RANK], axis=-1)
    q = (rms_norm(c_q, q_norm_g) @ w_qb).reshape(B, S, H, MLA_QK_DIM)
    kv = (rms_norm(c_kv, kv_norm_g) @ w_kvb).reshape(B, S, H, MLA_NOPE_DIM + MLA_V_DIM)
    k_nope, v = kv[..., :MLA_NOPE_DIM], kv[..., MLA_NOPE_DIM:]
    k_rope = jnp.broadcast_to(k_rope[:, :, None, :], (B, S, H, MLA_ROPE_DIM))
    k = jnp.concatenate([k_nope, k_rope], axis=-1)
    q = rms_norm(q, q_head_g)
    k = rms_norm(k, k_head_g)
    pos = jnp.arange(S)
    q = jnp.concatenate([q[..., :MLA_NOPE_DIM], rope(q[..., MLA_NOPE_DIM:], pos)], axis=-1)
    k = jnp.concatenate([k[..., :MLA_NOPE_DIM], rope(k[..., MLA_NOPE_DIM:], pos)], axis=-1)
    q = q * (MLA_QK_DIM ** -0.5)
    chunk_id = jnp.arange(S) // CHUNK
    outs = []
    for blk in range(S // Q_BLOCK):
        start, stop = blk * Q_BLOCK, (blk + 1) * Q_BLOCK
        logits = jnp.einsum('bqhd,bkhd->bhqk', q[:, start:stop], k[:, :stop]).astype(jnp.float32)
        mask = chunk_id[None, :stop] <= chunk_id[start:stop, None]
        logits = jnp.where(mask, logits, MASK_VALUE)
        p = jax.nn.softmax(logits, axis=-1).astype(v.dtype)
        outs.append(jnp.einsum('bhqk,bkhd->bqhd', p, v[:, :stop]))
    o = jnp.concatenate(outs, axis=1).reshape(B, S, H * MLA_V_DIM)
    return o @ w_out


def conv_ffn(h, w_in, conv_w, conv_b, w_out):
    a, g = jnp.split(h @ w_in, 2, axis=-1)
    g = lax.conv_general_dilated(
        g, conv_w[:, None, :], window_strides=(1,), padding=[(CONV_WIDTH - 1, 0)],
        dimension_numbers=('NWC', 'WIO', 'NWC'), feature_group_count=FFN_DIM) + conv_b
    return (jax.nn.silu(g) * a) @ w_out


def setup_inputs(seed: int = 0) -> dict:
    key = jax.random.key(seed)
    ks = jax.random.split(key, 20)

    def dense(k, lead, fan_in, fan_out):
        return jax.random.normal(k, (lead, fan_in, fan_out), jnp.float32) * (fan_in ** -0.5)

    def gain(k, shape):
        return 1.0 + 0.01 * jax.random.normal(k, shape, jnp.float32)

    R, M, L = N_RET_LAYERS, N_MLA_LAYERS, DEPTH
    ret_in_width = 2 * RET_HEADS * RET_QK_DIM + 2 * RET_HEADS * RET_V_DIM
    mla_in_width = MLA_Q_RANK + MLA_KV_RANK + MLA_ROPE_DIM
    return {
        "x": jax.random.normal(ks[0], (BATCH, SEQ, D_MODEL), jnp.float32),
        "ret_norm": gain(ks[1], (R, D_MODEL)),
        "ret_w_in": dense(ks[2], R, D_MODEL, ret_in_width),
        "ret_gn": gain(ks[3], (R, RET_HEADS, RET_V_DIM)),
        "ret_w_out": dense(ks[4], R, RET_HEADS * RET_V_DIM, D_MODEL),
        "mla_norm": gain(ks[5], (M, D_MODEL)),
        "mla_w_in": dense(ks[6], M, D_MODEL, mla_in_width),
        "mla_q_norm": gain(ks[7], (M, MLA_Q_RANK)),
        "mla_w_qb": dense(ks[8], M, MLA_Q_RANK, MLA_HEADS * MLA_QK_DIM),
        "mla_kv_norm": gain(ks[9], (M, MLA_KV_RANK)),
        "mla_w_kvb": dense(ks[10], M, MLA_KV_RANK, MLA_HEADS * (MLA_NOPE_DIM + MLA_V_DIM)),
        "mla_q_head_norm": gain(ks[11], (M, MLA_QK_DIM)),
        "mla_k_head_norm": gain(ks[12], (M, MLA_QK_DIM)),
        "mla_w_out": dense(ks[13], M, MLA_HEADS * MLA_V_DIM, D_MODEL),
        "ffn_norm": gain(ks[14], (L, D_MODEL)),
        "ffn_w_in": dense(ks[15], L, D_MODEL, 2 * FFN_DIM),
        "ffn_conv_w": jax.random.normal(ks[16], (L, CONV_WIDTH, FFN_DIM), jnp.float32) * (CONV_WIDTH ** -0.5),
        "ffn_conv_b": 0.01 * jax.random.normal(ks[17], (L, FFN_DIM), jnp.float32),
        "ffn_w_out": dense(ks[18], L, FFN_DIM, D_MODEL),
    }


def reference(x, ret_norm, ret_w_in, ret_gn, ret_w_out, mla_norm, mla_w_in, mla_q_norm, mla_w_qb,
              mla_kv_norm, mla_w_kvb, mla_q_head_norm, mla_k_head_norm, mla_w_out,
              ffn_norm, ffn_w_in, ffn_conv_w, ffn_conv_b, ffn_w_out):
    for i in range(DEPTH):
        j = i // N_MIXERS
        if i % N_MIXERS == 0:
            x = x + retention_mixer(rms_norm(x, ret_norm[j]), ret_w_in[j], ret_gn[j], ret_w_out[j])
        else:
            x = x + mla_mixer(rms_norm(x, mla_norm[j]), mla_w_in[j], mla_q_norm[j], mla_w_qb[j],
                              mla_kv_norm[j], mla_w_kvb[j], mla_q_head_norm[j], mla_k_head_norm[j],
                              mla_w_out[j])
        x = x + conv_ffn(rms_norm(x, ffn_norm[i]), ffn_w_in[i], ffn_conv_w[i], ffn_conv_b[i], ffn_w_out[i])
    return x
```

```python
import functools

import jax
import jax.numpy as jnp
from jax import lax
from jax.experimental import pallas as pl
from jax.experimental.pallas import tpu as pltpu

F32 = jnp.float32
BF16 = jnp.bfloat16

CHUNK = 64
RMS_EPS = 1e-6
ROPE_THETA = 10000.0
RET_HEADS = 4
RET_QK_DIM = 256
RET_V_DIM = 512
RET_GAMMA_BASE = -5.0
MLA_HEADS = 8
MLA_Q_RANK = 384
MLA_KV_RANK = 256
MLA_NOPE_DIM = 128
MLA_ROPE_DIM = 64
MLA_V_DIM = 128
MLA_QK_DIM = MLA_NOPE_DIM + MLA_ROPE_DIM
MLA_HEAD_PAD = 256
MASK_VALUE = -1e30
CONV_WIDTH = 3

LANES = 128
SUBLANES = 8
VMEM_LIMIT_BYTES = 52 * 1024 * 1024

RET_SUPER = 256
PROJ_TM = 1024
PROJ_TN = 1024
OUT_TM = 512
FFN_TM = 512
FFN_SPLIT = 2
MLA_TM = 512
ATT_TQ = 256
ATT_TK = 256


def _params(*sem):
    return pltpu.CompilerParams(dimension_semantics=sem, vmem_limit_bytes=VMEM_LIMIT_BYTES)


def _rms_scale(x):
    return lax.rsqrt(jnp.mean(x * x, axis=-1, keepdims=True) + RMS_EPS)


def _sigmoid(x):
    return 1.0 / (1.0 + jnp.exp(-x))


def _dot(a, b):
    return jnp.dot(a, b, preferred_element_type=F32)


def _dot_nt(a, b):
    return lax.dot_general(a, b, (((1,), (1,)), ((), ())), preferred_element_type=F32)


def _dot_tn(a, b):
    return lax.dot_general(a, b, (((0,), (0,)), ((), ())), preferred_element_type=F32)


def _ret_in_kernel(x_ref, gain_ref, w_ref, cos_ref, sin_ref, o_ref, h_ref, *, n_q_tiles, n_rope_tiles):
    j = pl.program_id(1)

    @pl.when(j == 0)
    def _():
        x = x_ref[...]
        h_ref[...] = (x * _rms_scale(x) * gain_ref[...]).astype(BF16)

    acc = _dot(h_ref[...], w_ref[...])
    half = RET_QK_DIM // 2

    @pl.when(j < n_rope_tiles)
    def _():
        scale = jnp.where(j >= n_q_tiles, RET_QK_DIM ** -0.5, 1.0).astype(F32)
        cos = cos_ref[...] * scale
        sin = sin_ref[...] * scale
        for hh in range(acc.shape[1] // RET_QK_DIM):
            lo = hh * RET_QK_DIM
            x1 = acc[:, lo:lo + half]
            x2 = acc[:, lo + half:lo + RET_QK_DIM]
            o_ref[:, lo:lo + half] = (x1 * cos - x2 * sin).astype(BF16)
            o_ref[:, lo + half:lo + RET_QK_DIM] = (x2 * cos + x1 * sin).astype(BF16)

    @pl.when(j >= n_rope_tiles)
    def _():
        o_ref[...] = acc.astype(BF16)


def _ret_in_proj(x2d, gain, w, cos, sin, seq):
    t, d = x2d.shape
    n = w.shape[1]
    tm, tn = PROJ_TM, PROJ_TN
    tps = seq // tm
    qk_cols = RET_HEADS * RET_QK_DIM
    kern = functools.partial(_ret_in_kernel, n_q_tiles=qk_cols // tn, n_rope_tiles=2 * qk_cols // tn)
    return pl.pallas_call(
        kern,
        grid=(t // tm, n // tn),
        in_specs=[
            pl.BlockSpec((tm, d), lambda i, j: (i, 0)),
            pl.BlockSpec((1, d), lambda i, j: (0, 0)),
            pl.BlockSpec((d, tn), lambda i, j: (0, j)),
            pl.BlockSpec((tm, RET_QK_DIM // 2), lambda i, j: (i % tps, 0)),
            pl.BlockSpec((tm, RET_QK_DIM // 2), lambda i, j: (i % tps, 0)),
        ],
        out_specs=pl.BlockSpec((tm, tn), lambda i, j: (i, j)),
        out_shape=jax.ShapeDtypeStruct((t, n), BF16),
        scratch_shapes=[pltpu.VMEM((tm, d), BF16)],
        compiler_params=_params("parallel", "arbitrary"),
        name="ret_in_proj",
    )(x2d, gain, w, cos, sin)


def _ret_core_kernel(q_ref, k_ref, v_ref, g_ref, dmat_ref, qd_ref, kd_ref, gn_ref, o_ref, state_ref, *, n_steps):
    state_ref[...] = jnp.zeros_like(state_ref)
    qd = qd_ref[0]
    kd = kd_ref[0]
    step_decay = qd[RET_SUPER - 1:RET_SUPER, :]
    gn = gn_ref[0]

    def body(sc, carry):
        r0 = pl.multiple_of(sc * RET_SUPER, RET_SUPER)
        rows = pl.ds(r0, RET_SUPER)
        q = q_ref[0, rows, :]
        k = k_ref[0, rows, :]
        v = v_ref[0, rows, :]
        scores = _dot_nt(q, k) * dmat_ref[0]
        inner = _dot(scores.astype(BF16), v)
        state = state_ref[...]
        q_scaled = (q.astype(F32) * qd).astype(BF16)
        cross = _dot(q_scaled, state.astype(BF16))
        k_scaled = (k.astype(F32) * kd).astype(BF16)
        state_ref[...] = state * step_decay + _dot_tn(k_scaled, v)
        out = inner + cross
        out = out * _rms_scale(out) * gn
        g = g_ref[0, rows, :].astype(F32)
        o_ref[0, rows, :] = (out * (g * _sigmoid(g))).astype(BF16)
        return carry

    lax.fori_loop(0, n_steps, body, 0)


def _ret_core(qkvg, dmat, qd, kd, gn, batch, seq):
    hd, dk, dv = RET_HEADS, RET_QK_DIM, RET_V_DIM
    qkvg3 = qkvg.reshape(batch, seq, qkvg.shape[-1])
    k_blk0 = hd * dk // dk
    v_blk0 = 2 * hd * dk // dv
    g_blk0 = (2 * hd * dk + hd * dv) // dv
    kern = functools.partial(_ret_core_kernel, n_steps=seq // RET_SUPER)
    return pl.pallas_call(
        kern,
        grid=(batch, hd),
        in_specs=[
            pl.BlockSpec((1, seq, dk), lambda b, h: (b, 0, h)),
            pl.BlockSpec((1, seq, dk), lambda b, h: (b, 0, k_blk0 + h)),
            pl.BlockSpec((1, seq, dv), lambda b, h: (b, 0, v_blk0 + h)),
            pl.BlockSpec((1, seq, dv), lambda b, h: (b, 0, g_blk0 + h)),
            pl.BlockSpec((1, RET_SUPER, RET_SUPER), lambda b, h: (h, 0, 0)),
            pl.BlockSpec((1, RET_SUPER, 1), lambda b, h: (h, 0, 0)),
            pl.BlockSpec((1, RET_SUPER, 1), lambda b, h: (h, 0, 0)),
            pl.BlockSpec((1, 1, dv), lambda b, h: (h, 0, 0)),
        ],
        out_specs=pl.BlockSpec((1, seq, dv), lambda b, h: (b, 0, h)),
        out_shape=jax.ShapeDtypeStruct((batch, seq, hd * dv), BF16),
        scratch_shapes=[pltpu.VMEM((dk, dv), F32)],
        compiler_params=_params("parallel", "parallel"),
        name="ret_core",
    )(qkvg3, qkvg3, qkvg3, qkvg3, dmat, qd, kd, gn)


def _out_proj_kernel(a_ref, w_ref, res_ref, o_ref):
    o_ref[...] = res_ref[...] + _dot(a_ref[...], w_ref[...])


def _out_proj(a, w, res):
    t, k = a.shape
    n = w.shape[1]
    tm = OUT_TM
    return pl.pallas_call(
        _out_proj_kernel,
        grid=(t // tm,),
        in_specs=[
            pl.BlockSpec((tm, k), lambda i: (i, 0)),
            pl.BlockSpec((k, n), lambda i: (0, 0)),
            pl.BlockSpec((tm, n), lambda i: (i, 0)),
        ],
        out_specs=pl.BlockSpec((tm, n), lambda i: (i, 0)),
        out_shape=jax.ShapeDtypeStruct((t, n), F32),
        compiler_params=_params("parallel"),
        name="out_proj",
    )(a, w, res)


def _ffn_kernel(x_ref, gain_ref, wa_ref, wg_ref, cw_ref, cb_ref, wo_ref, o_ref, h_ref, carry_ref, *, tiles_per_seq):
    i = pl.program_id(0)
    j = pl.program_id(1)
    tm = x_ref.shape[0]

    @pl.when(j == 0)
    def _():
        x = x_ref[...]
        h_ref[...] = (x * _rms_scale(x) * gain_ref[...]).astype(BF16)

    h = h_ref[...]
    a = _dot(h, wa_ref[...])
    g = _dot(h, wg_ref[...])

    prev = carry_ref[j]
    prev = jnp.where(i % tiles_per_seq == 0, 0.0, prev)
    prev1 = prev[SUBLANES - 1:SUBLANES, :]
    prev2 = prev[SUBLANES - 2:SUBLANES - 1, :]
    row = lax.broadcasted_iota(jnp.int32, (tm, 1), 0)
    g1 = jnp.where(row == 0, prev1, pltpu.roll(g, 1, 0))
    g2 = jnp.where(row == 0, prev2, jnp.where(row == 1, prev1, pltpu.roll(g, 2, 0)))
    carry_ref[j] = g[tm - SUBLANES:tm, :]
    cw = cw_ref[...]
    gc = g2 * cw[0:1, :] + g1 * cw[1:2, :] + g * cw[2:3, :] + cb_ref[...]
    act = (gc * _sigmoid(gc) * a).astype(BF16)
    contrib = _dot(act, wo_ref[...])

    @pl.when(j == 0)
    def _():
        o_ref[...] = x_ref[...] + contrib

    @pl.when(j > 0)
    def _():
        o_ref[...] += contrib


def _ffn(x2d, gain, w_in, conv_w, conv_b, w_out, seq):
    t, d = x2d.shape
    f = w_out.shape[0]
    tm = FFN_TM
    tf = f // FFN_SPLIT
    kern = functools.partial(_ffn_kernel, tiles_per_seq=seq // tm)
    return pl.pallas_call(
        kern,
        grid=(t // tm, FFN_SPLIT),
        in_specs=[
            pl.BlockSpec((tm, d), lambda i, j: (i, 0)),
            pl.BlockSpec((1, d), lambda i, j: (0, 0)),
            pl.BlockSpec((d, tf), lambda i, j: (0, j)),
            pl.BlockSpec((d, tf), lambda i, j: (0, FFN_SPLIT + j)),
            pl.BlockSpec((CONV_WIDTH, tf), lambda i, j: (0, j)),
            pl.BlockSpec((1, tf), lambda i, j: (0, j)),
            pl.BlockSpec((tf, d), lambda i, j: (j, 0)),
        ],
        out_specs=pl.BlockSpec((tm, d), lambda i, j: (i, 0)),
        out_shape=jax.ShapeDtypeStruct((t, d), F32),
        scratch_shapes=[pltpu.VMEM((tm, d), BF16), pltpu.VMEM((FFN_SPLIT, SUBLANES, tf), F32)],
        compiler_params=_params("arbitrary", "arbitrary"),
        name="conv_ffn",
    )(x2d, gain, w_in, w_in, conv_w, conv_b, w_out)


def _mla_proj_kernel(x_ref, gain_ref, w_in_ref, qn_ref, kvn_ref, wq_ref, wqs_ref, wkv_ref,
                     c_ref, s_ref, gq_ref, gk_ref, q_ref, k_ref, v_ref):
    x = x_ref[...]
    h = (x * _rms_scale(x) * gain_ref[...]).astype(BF16)
    p = _dot(h, w_in_ref[...])
    c_q = p[:, :MLA_Q_RANK]
    c_kv = p[:, MLA_Q_RANK:MLA_Q_RANK + MLA_KV_RANK]
    k_r = p[:, MLA_Q_RANK + MLA_KV_RANK:MLA_Q_RANK + MLA_KV_RANK + LANES]
    k_rs = p[:, MLA_Q_RANK + MLA_KV_RANK + LANES:]
    c_q = (c_q * _rms_scale(c_q) * qn_ref[...]).astype(BF16)
    c_kv = (c_kv * _rms_scale(c_kv) * kvn_ref[...]).astype(BF16)
    q_all = _dot(c_q, wq_ref[...])
    q_swp = _dot(c_q, wqs_ref[...])
    kv = _dot(c_kv, wkv_ref[...])

    cos = c_ref[...]
    sin = s_ref[...]
    gq_nope, gq_rope, gq_swap = gq_ref[0:1, :], gq_ref[1:2, :], gq_ref[2:3, :]
    gk_nope, gk_rope, gk_swap = gk_ref[0:1, :], gk_ref[1:2, :], gk_ref[2:3, :]
    inv_dim = 1.0 / MLA_QK_DIM
    q_scale = MLA_QK_DIM ** -0.5

    k_rope_ss = jnp.sum(k_r * k_r, axis=-1, keepdims=True)
    k_rope_rot = k_r * gk_rope * cos + k_rs * gk_swap * sin

    for hh in range(MLA_HEADS):
        lo = hh * MLA_HEAD_PAD
        q_n = q_all[:, lo:lo + LANES]
        q_r = q_all[:, lo + LANES:lo + 2 * LANES]
        q_s = q_swp[:, hh * LANES:(hh + 1) * LANES]
        ss = jnp.sum(q_n * q_n, axis=-1, keepdims=True) + jnp.sum(q_r * q_r, axis=-1, keepdims=True)
        r = lax.rsqrt(ss * inv_dim + RMS_EPS) * q_scale
        q_ref[0, hh, :, 0:LANES] = (q_n * r * gq_nope).astype(BF16)
        q_ref[0, hh, :, LANES:2 * LANES] = ((q_r * gq_rope * cos + q_s * gq_swap * sin) * r).astype(BF16)

        k_n = kv[:, lo:lo + LANES]
        ssk = jnp.sum(k_n * k_n, axis=-1, keepdims=True) + k_rope_ss
        rk = lax.rsqrt(ssk * inv_dim + RMS_EPS)
        k_ref[0, hh, :, 0:LANES] = (k_n * rk * gk_nope).astype(BF16)
        k_ref[0, hh, :, LANES:2 * LANES] = (k_rope_rot * rk).astype(BF16)
        v_ref[0, hh] = kv[:, lo + LANES:lo + 2 * LANES].astype(BF16)


def _mla_proj(x2d, gain, w_in, qn, kvn, wq, wqs, wkv, cos, sin, gq, gk, batch, seq):
    t, d = x2d.shape
    tm = MLA_TM
    tps = seq // tm
    hd = MLA_HEADS

    def full(arr):
        return pl.BlockSpec(arr.shape, lambda i: (0,) * arr.ndim)

    return pl.pallas_call(
        _mla_proj_kernel,
        grid=(t // tm,),
        in_specs=[
            pl.BlockSpec((tm, d), lambda i: (i, 0)),
            full(gain), full(w_in), full(qn), full(kvn), full(wq), full(wqs), full(wkv),
            pl.BlockSpec((tm, LANES), lambda i: (i % tps, 0)),
            pl.BlockSpec((tm, LANES), lambda i: (i % tps, 0)),
            full(gq), full(gk),
        ],
        out_specs=[
            pl.BlockSpec((1, hd, tm, MLA_HEAD_PAD), lambda i: (i // tps, 0, i % tps, 0)),
            pl.BlockSpec((1, hd, tm, MLA_HEAD_PAD), lambda i: (i // tps, 0, i % tps, 0)),
            pl.BlockSpec((1, hd, tm, MLA_V_DIM), lambda i: (i // tps, 0, i % tps, 0)),
        ],
        out_shape=[
            jax.ShapeDtypeStruct((batch, hd, seq, MLA_HEAD_PAD), BF16),
            jax.ShapeDtypeStruct((batch, hd, seq, MLA_HEAD_PAD), BF16),
            jax.ShapeDtypeStruct((batch, hd, seq, MLA_V_DIM), BF16),
        ],
        compiler_params=_params("parallel"),
        name="mla_proj",
    )(x2d, gain, w_in, qn, kvn, wq, wqs, wkv, cos, sin, gq, gk)


def _attn_kernel(q_ref, k_ref, v_ref, o_ref, *, n_q_blocks):
    tq, tk = ATT_TQ, ATT_TK

    def update(carry, q, k0, masked):
        m, l, acc = carry
        k = k_ref[0, 0, pl.ds(k0, tk), :]
        v = v_ref[0, 0, pl.ds(k0, tk), :]
        s = _dot_nt(q, k)
        if masked:
            row = lax.broadcasted_iota(jnp.int32, (tq, tk), 0) // CHUNK
            col = lax.broadcasted_iota(jnp.int32, (tq, tk), 1) // CHUNK
            s = jnp.where(col <= row, s, MASK_VALUE)
        m_new = jnp.maximum(m, jnp.max(s, axis=-1, keepdims=True))
        alpha = jnp.exp(m - m_new)
        p = jnp.exp(s - m_new)
        l = alpha * l + jnp.sum(p, axis=-1, keepdims=True)
        acc = alpha * acc + _dot(p.astype(BF16), v)
        return m_new, l, acc

    def q_body(qi, carry):
        q0 = pl.multiple_of(qi * tq, tq)
        q = q_ref[0, 0, pl.ds(q0, tq), :]
        init = (jnp.full((tq, 1), MASK_VALUE, F32), jnp.zeros((tq, 1), F32), jnp.zeros((tq, MLA_V_DIM), F32))
        state = lax.fori_loop(
            0, qi, lambda kb, c: update(c, q, pl.multiple_of(kb * tk, tk), False), init)
        _, l, acc = update(state, q, q0, True)
        o_ref[0, pl.ds(q0, tq), :] = (acc / l).astype(BF16)
        return carry

    lax.fori_loop(0, n_q_blocks, q_body, 0)


def _attention(q, k, v):
    batch, hd, seq, _ = q.shape
    kern = functools.partial(_attn_kernel, n_q_blocks=seq // ATT_TQ)
    return pl.pallas_call(
        kern,
        grid=(batch, hd),
        in_specs=[
            pl.BlockSpec((1, 1, seq, MLA_HEAD_PAD), lambda b, h: (b, h, 0, 0)),
            pl.BlockSpec((1, 1, seq, MLA_HEAD_PAD), lambda b, h: (b, h, 0, 0)),
            pl.BlockSpec((1, 1, seq, MLA_V_DIM), lambda b, h: (b, h, 0, 0)),
        ],
        out_specs=pl.BlockSpec((1, seq, MLA_V_DIM), lambda b, h: (b, 0, h)),
        out_shape=jax.ShapeDtypeStruct((batch, seq, hd * MLA_V_DIM), BF16),
        compiler_params=_params("parallel", "parallel"),
        name="mla_attention",
    )(q, k, v)


def _rope_tables(seq, half):
    inv_freq = ROPE_THETA ** (-jnp.arange(half, dtype=F32) / half)
    ang = jnp.arange(seq).astype(F32)[:, None] * inv_freq[None, :]
    return jnp.cos(ang), jnp.sin(ang)


def _retention_decay_tables():
    sc = RET_SUPER
    log_gamma = jnp.log1p(-jnp.exp2(RET_GAMMA_BASE - jnp.arange(RET_HEADS, dtype=F32)))
    idx = jnp.arange(sc, dtype=F32)
    dist = jnp.abs(idx[:, None] - idx[None, :])
    chunk = jnp.arange(sc) // CHUNK
    visible = chunk[None, :] <= chunk[:, None]
    dmat = jnp.where(visible[None], jnp.exp(log_gamma[:, None, None] * dist[None]), 0.0)
    qd = jnp.exp(log_gamma[:, None] * (idx + 1.0))[:, :, None]
    kd = jnp.exp(log_gamma[:, None] * (sc - 1.0 - idx))[:, :, None]
    return dmat, qd, kd


def _pad_cols(w, width):
    return jnp.pad(w, ((0, 0), (0, width - w.shape[1])))


def _swap_halves(w):
    half = w.shape[-1] // 2
    return jnp.concatenate([w[..., half:], w[..., :half]], axis=-1)


def _mla_weights(w_in, w_qb, q_head_g, k_head_g):
    rq, rkv = MLA_Q_RANK, MLA_KV_RANK
    w_kr = w_in[:, rq + rkv:]
    w_in_l = jnp.concatenate(
        [w_in[:, :rq + rkv], _pad_cols(w_kr, LANES), _pad_cols(_swap_halves(w_kr), LANES)], axis=1)
    w_qb3 = w_qb.reshape(rq, MLA_HEADS, MLA_QK_DIM)
    nope, rope = w_qb3[..., :MLA_NOPE_DIM], w_qb3[..., MLA_NOPE_DIM:]
    zeros = jnp.zeros((rq, MLA_HEADS, LANES - MLA_ROPE_DIM), w_qb.dtype)
    wq = jnp.concatenate([nope, rope, zeros], axis=-1).reshape(rq, MLA_HEADS * MLA_HEAD_PAD)
    wqs = jnp.concatenate([_swap_halves(rope), zeros], axis=-1).reshape(rq, MLA_HEADS * LANES)

    def gains(g):
        rope_g = g[MLA_NOPE_DIM:]
        pad = jnp.zeros((LANES - MLA_ROPE_DIM,), g.dtype)
        return jnp.stack([g[:MLA_NOPE_DIM], jnp.concatenate([rope_g, pad]),
                          jnp.concatenate([_swap_halves(rope_g), pad])])

    return w_in_l, wq, wqs, gains(q_head_g), gains(k_head_g)


def kernel(x, ret_norm, ret_w_in, ret_gn, ret_w_out, mla_norm, mla_w_in, mla_q_norm, mla_w_qb, mla_kv_norm,
           mla_w_kvb, mla_q_head_norm, mla_k_head_norm, mla_w_out, ffn_norm, ffn_w_in, ffn_conv_w, ffn_conv_b,
           ffn_w_out):
    batch, seq, d = x.shape
    x2d = x.reshape(batch * seq, d)

    cos_r, sin_r = _rope_tables(seq, RET_QK_DIM // 2)
    dmat, qd, kd = _retention_decay_tables()
    qkvg = _ret_in_proj(x2d, ret_norm[0][None, :], ret_w_in[0].astype(BF16), cos_r, sin_r, seq)
    ret = _ret_core(qkvg, dmat, qd, kd, ret_gn[0][:, None, :], batch, seq)
    x2d = _out_proj(ret.reshape(batch * seq, -1), ret_w_out[0].astype(BF16), x2d)
    x2d = _ffn(x2d, ffn_norm[0][None, :], ffn_w_in[0].astype(BF16), ffn_conv_w[0], ffn_conv_b[0][None, :],
               ffn_w_out[0].astype(BF16), seq)

    cos_m, sin_m = _rope_tables(seq, MLA_ROPE_DIM // 2)
    pad = jnp.zeros((seq, LANES - MLA_ROPE_DIM), F32)
    cos_t = jnp.concatenate([cos_m, cos_m, pad], axis=1)
    sin_t = jnp.concatenate([-sin_m, sin_m, pad], axis=1)
    w_in_l, wq, wqs, gq, gk = _mla_weights(mla_w_in[0], mla_w_qb[0], mla_q_head_norm[0], mla_k_head_norm[0])
    q, k, v = _mla_proj(x2d, mla_norm[0][None, :], w_in_l.astype(BF16), mla_q_norm[0][None, :],
                        mla_kv_norm[0][None, :], wq.astype(BF16), wqs.astype(BF16), mla_w_kvb[0].astype(BF16),
                        cos_t, sin_t, gq, gk, batch, seq)
    att = _attention(q, k, v)
    x2d = _out_proj(att.reshape(batch * seq, -1), mla_w_out[0].astype(BF16), x2d)
    x2d = _ffn(x2d, ffn_norm[1][None, :], ffn_w_in[1].astype(BF16), ffn_conv_w[1], ffn_conv_b[1][None, :],
               ffn_w_out[1].astype(BF16), seq)
    return x2d.reshape(batch, seq, d)
```

```python
import functools

import jax
import jax.numpy as jnp
from jax import lax
from jax.experimental import pallas as pl
from jax.experimental.pallas import tpu as pltpu

F32 = jnp.float32
BF16 = jnp.bfloat16

CHUNK = 64
RMS_EPS = 1e-6
ROPE_THETA = 10000.0
RET_HEADS = 4
RET_QK_DIM = 256
RET_V_DIM = 512
RET_GAMMA_BASE = -5.0
MLA_HEADS = 8
MLA_Q_RANK = 384
MLA_KV_RANK = 256
MLA_NOPE_DIM = 128
MLA_ROPE_DIM = 64
MLA_V_DIM = 128
MLA_QK_DIM = MLA_NOPE_DIM + MLA_ROPE_DIM
MLA_HEAD_PAD = 256
MASK_VALUE = -1e30
CONV_WIDTH = 3

LANES = 128
SUBLANES = 8
VMEM_LIMIT_BYTES = 52 * 1024 * 1024

RET_SUPER = 256
PROJ_TM = 1024
PROJ_TN = 1024
OUT_TM = 512
FFN_TM = 512
FFN_SPLIT = 2
MLA_TM = 512
ATT_TQ = 256


def _params(*sem):
    return pltpu.CompilerParams(dimension_semantics=sem, vmem_limit_bytes=VMEM_LIMIT_BYTES)


def _rms_scale(x):
    return lax.rsqrt(jnp.mean(x * x, axis=-1, keepdims=True) + RMS_EPS)


def _sigmoid(x):
    return 1.0 / (1.0 + jnp.exp(-x))


def _dot(a, b):
    return jnp.dot(a, b, preferred_element_type=F32)


def _dot_nt(a, b):
    return lax.dot_general(a, b, (((1,), (1,)), ((), ())), preferred_element_type=F32)


def _dot_tn(a, b):
    return lax.dot_general(a, b, (((0,), (0,)), ((), ())), preferred_element_type=F32)


def _ret_in_kernel(x_ref, gain_ref, w_ref, cos_ref, sin_ref, o_ref, h_ref, *, n_q_tiles, n_rope_tiles):
    j = pl.program_id(1)

    @pl.when(j == 0)
    def _():
        x = x_ref[...]
        h_ref[...] = (x * _rms_scale(x) * gain_ref[...]).astype(BF16)

    acc = _dot(h_ref[...], w_ref[...])
    half = RET_QK_DIM // 2

    @pl.when(j < n_rope_tiles)
    def _():
        scale = jnp.where(j >= n_q_tiles, RET_QK_DIM ** -0.5, 1.0).astype(F32)
        cos = cos_ref[...] * scale
        sin = sin_ref[...] * scale
        for hh in range(acc.shape[1] // RET_QK_DIM):
            lo = hh * RET_QK_DIM
            x1 = acc[:, lo:lo + half]
            x2 = acc[:, lo + half:lo + RET_QK_DIM]
            o_ref[:, lo:lo + half] = (x1 * cos - x2 * sin).astype(BF16)
            o_ref[:, lo + half:lo + RET_QK_DIM] = (x2 * cos + x1 * sin).astype(BF16)

    @pl.when(j >= n_rope_tiles)
    def _():
        o_ref[...] = acc.astype(BF16)


def _ret_in_proj(x2d, gain, w, cos, sin, seq):
    t, d = x2d.shape
    n = w.shape[1]
    tm, tn = PROJ_TM, PROJ_TN
    tps = seq // tm
    qk_cols = RET_HEADS * RET_QK_DIM
    kern = functools.partial(_ret_in_kernel, n_q_tiles=qk_cols // tn, n_rope_tiles=2 * qk_cols // tn)
    return pl.pallas_call(
        kern,
        grid=(t // tm, n // tn),
        in_specs=[
            pl.BlockSpec((tm, d), lambda i, j: (i, 0)),
            pl.BlockSpec((1, d), lambda i, j: (0, 0)),
            pl.BlockSpec((d, tn), lambda i, j: (0, j)),
            pl.BlockSpec((tm, RET_QK_DIM // 2), lambda i, j: (i % tps, 0)),
            pl.BlockSpec((tm, RET_QK_DIM // 2), lambda i, j: (i % tps, 0)),
        ],
        out_specs=pl.BlockSpec((tm, tn), lambda i, j: (i, j)),
        out_shape=jax.ShapeDtypeStruct((t, n), BF16),
        scratch_shapes=[pltpu.VMEM((tm, d), BF16)],
        compiler_params=_params("parallel", "arbitrary"),
        name="ret_in_proj",
    )(x2d, gain, w, cos, sin)


def _ret_core_kernel(q_ref, k_ref, v_ref, g_ref, dmat_ref, qd_ref, kd_ref, gn_ref, o_ref, state_ref, *, n_steps):
    state_ref[...] = jnp.zeros_like(state_ref)
    qd = qd_ref[0]
    kd = kd_ref[0]
    step_decay = qd[RET_SUPER - 1:RET_SUPER, :]
    gn = gn_ref[0]

    def body(sc, carry):
        r0 = pl.multiple_of(sc * RET_SUPER, RET_SUPER)
        rows = pl.ds(r0, RET_SUPER)
        q = q_ref[0, rows, :]
        k = k_ref[0, rows, :]
        v = v_ref[0, rows, :]
        scores = _dot_nt(q, k) * dmat_ref[0]
        inner = _dot(scores.astype(BF16), v)
        state = state_ref[...]
        q_scaled = (q.astype(F32) * qd).astype(BF16)
        cross = _dot(q_scaled, state.astype(BF16))
        k_scaled = (k.astype(F32) * kd).astype(BF16)
        state_ref[...] = state * step_decay + _dot_tn(k_scaled, v)
        out = inner + cross
        out = out * _rms_scale(out) * gn
        g = g_ref[0, rows, :].astype(F32)
        o_ref[0, rows, :] = (out * (g * _sigmoid(g))).astype(BF16)
        return carry

    lax.fori_loop(0, n_steps, body, 0)


def _ret_core(qkvg, dmat, qd, kd, gn, batch, seq):
    hd, dk, dv = RET_HEADS, RET_QK_DIM, RET_V_DIM
    qkvg3 = qkvg.reshape(batch, seq, qkvg.shape[-1])
    k_blk0 = hd * dk // dk
    v_blk0 = 2 * hd * dk // dv
    g_blk0 = (2 * hd * dk + hd * dv) // dv
    kern = functools.partial(_ret_core_kernel, n_steps=seq // RET_SUPER)
    return pl.pallas_call(
        kern,
        grid=(batch, hd),
        in_specs=[
            pl.BlockSpec((1, seq, dk), lambda b, h: (b, 0, h)),
            pl.BlockSpec((1, seq, dk), lambda b, h: (b, 0, k_blk0 + h)),
            pl.BlockSpec((1, seq, dv), lambda b, h: (b, 0, v_blk0 + h)),
            pl.BlockSpec((1, seq, dv), lambda b, h: (b, 0, g_blk0 + h)),
            pl.BlockSpec((1, RET_SUPER, RET_SUPER), lambda b, h: (h, 0, 0)),
            pl.BlockSpec((1, RET_SUPER, 1), lambda b, h: (h, 0, 0)),
            pl.BlockSpec((1, RET_SUPER, 1), lambda b, h: (h, 0, 0)),
            pl.BlockSpec((1, 1, dv), lambda b, h: (h, 0, 0)),
        ],
        out_specs=pl.BlockSpec((1, seq, dv), lambda b, h: (b, 0, h)),
        out_shape=jax.ShapeDtypeStruct((batch, seq, hd * dv), BF16),
        scratch_shapes=[pltpu.VMEM((dk, dv), F32)],
        compiler_params=_params("parallel", "parallel"),
        name="ret_core",
    )(qkvg3, qkvg3, qkvg3, qkvg3, dmat, qd, kd, gn)


def _out_proj_kernel(a_ref, w_ref, res_ref, o_ref):
    o_ref[...] = res_ref[...] + _dot(a_ref[...], w_ref[...])


def _out_proj(a, w, res):
    t, k = a.shape
    n = w.shape[1]
    tm = OUT_TM
    return pl.pallas_call(
        _out_proj_kernel,
        grid=(t // tm,),
        in_specs=[
            pl.BlockSpec((tm, k), lambda i: (i, 0)),
            pl.BlockSpec((k, n), lambda i: (0, 0)),
            pl.BlockSpec((tm, n), lambda i: (i, 0)),
        ],
        out_specs=pl.BlockSpec((tm, n), lambda i: (i, 0)),
        out_shape=jax.ShapeDtypeStruct((t, n), F32),
        compiler_params=_params("parallel"),
        name="out_proj",
    )(a, w, res)


def _ffn_kernel(x_ref, gain_ref, wa_ref, wg_ref, cw_ref, cb_ref, wo_ref, o_ref, h_ref, carry_ref, *, tiles_per_seq):
    i = pl.program_id(0)
    j = pl.program_id(1)
    tm = x_ref.shape[0]

    @pl.when(j == 0)
    def _():
        x = x_ref[...]
        h_ref[...] = (x * _rms_scale(x) * gain_ref[...]).astype(BF16)

    h = h_ref[...]
    a = _dot(h, wa_ref[...])
    g = _dot(h, wg_ref[...])

    prev = carry_ref[j]
    prev = jnp.where(i % tiles_per_seq == 0, 0.0, prev)
    prev1 = prev[SUBLANES - 1:SUBLANES, :]
    prev2 = prev[SUBLANES - 2:SUBLANES - 1, :]
    row = lax.broadcasted_iota(jnp.int32, (tm, 1), 0)
    g1 = jnp.where(row == 0, prev1, pltpu.roll(g, 1, 0))
    g2 = jnp.where(row == 0, prev2, jnp.where(row == 1, prev1, pltpu.roll(g, 2, 0)))
    carry_ref[j] = g[tm - SUBLANES:tm, :]
    cw = cw_ref[...]
    gc = g2 * cw[0:1, :] + g1 * cw[1:2, :] + g * cw[2:3, :] + cb_ref[...]
    act = (gc * _sigmoid(gc) * a).astype(BF16)
    contrib = _dot(act, wo_ref[...])

    @pl.when(j == 0)
    def _():
        o_ref[...] = x_ref[...] + contrib

    @pl.when(j > 0)
    def _():
        o_ref[...] += contrib


def _ffn(x2d, gain, w_in, conv_w, conv_b, w_out, seq):
    t, d = x2d.shape
    f = w_out.shape[0]
    tm = FFN_TM
    tf = f // FFN_SPLIT
    kern = functools.partial(_ffn_kernel, tiles_per_seq=seq // tm)
    return pl.pallas_call(
        kern,
        grid=(t // tm, FFN_SPLIT),
        in_specs=[
            pl.BlockSpec((tm, d), lambda i, j: (i, 0)),
            pl.BlockSpec((1, d), lambda i, j: (0, 0)),
            pl.BlockSpec((d, tf), lambda i, j: (0, j)),
            pl.BlockSpec((d, tf), lambda i, j: (0, FFN_SPLIT + j)),
            pl.BlockSpec((CONV_WIDTH, tf), lambda i, j: (0, j)),
            pl.BlockSpec((1, tf), lambda i, j: (0, j)),
            pl.BlockSpec((tf, d), lambda i, j: (j, 0)),
        ],
        out_specs=pl.BlockSpec((tm, d), lambda i, j: (i, 0)),
        out_shape=jax.ShapeDtypeStruct((t, d), F32),
        scratch_shapes=[pltpu.VMEM((tm, d), BF16), pltpu.VMEM((FFN_SPLIT, SUBLANES, tf), F32)],
        compiler_params=_params("arbitrary", "arbitrary"),
        name="conv_ffn",
    )(x2d, gain, w_in, w_in, conv_w, conv_b, w_out)


def _mla_proj_kernel(x_ref, gain_ref, w_in_ref, qn_ref, kvn_ref, wq_ref, wqs_ref, wkv_ref,
                     c_ref, s_ref, gq_ref, gk_ref, q_ref, k_ref, v_ref):
    x = x_ref[...]
    h = (x * _rms_scale(x) * gain_ref[...]).astype(BF16)
    p = _dot(h, w_in_ref[...])
    c_q = p[:, :MLA_Q_RANK]
    c_kv = p[:, MLA_Q_RANK:MLA_Q_RANK + MLA_KV_RANK]
    k_r = p[:, MLA_Q_RANK + MLA_KV_RANK:MLA_Q_RANK + MLA_KV_RANK + LANES]
    k_rs = p[:, MLA_Q_RANK + MLA_KV_RANK + LANES:]
    c_q = (c_q * _rms_scale(c_q) * qn_ref[...]).astype(BF16)
    c_kv = (c_kv * _rms_scale(c_kv) * kvn_ref[...]).astype(BF16)
    q_all = _dot(c_q, wq_ref[...])
    q_swp = _dot(c_q, wqs_ref[...])
    kv = _dot(c_kv, wkv_ref[...])

    cos = c_ref[...]
    sin = s_ref[...]
    gq_nope, gq_rope, gq_swap = gq_ref[0:1, :], gq_ref[1:2, :], gq_ref[2:3, :]
    gk_nope, gk_rope, gk_swap = gk_ref[0:1, :], gk_ref[1:2, :], gk_ref[2:3, :]
    inv_dim = 1.0 / MLA_QK_DIM
    q_scale = MLA_QK_DIM ** -0.5

    k_rope_ss = jnp.sum(k_r * k_r, axis=-1, keepdims=True)
    k_rope_rot = k_r * gk_rope * cos + k_rs * gk_swap * sin

    for hh in range(MLA_HEADS):
        lo = hh * MLA_HEAD_PAD
        q_n = q_all[:, lo:lo + LANES]
        q_r = q_all[:, lo + LANES:lo + 2 * LANES]
        q_s = q_swp[:, hh * LANES:(hh + 1) * LANES]
        ss = jnp.sum(q_n * q_n, axis=-1, keepdims=True) + jnp.sum(q_r * q_r, axis=-1, keepdims=True)
        r = lax.rsqrt(ss * inv_dim + RMS_EPS) * q_scale
        q_ref[0, hh, :, 0:LANES] = (q_n * r * gq_nope).astype(BF16)
        q_ref[0, hh, :, LANES:2 * LANES] = ((q_r * gq_rope * cos + q_s * gq_swap * sin) * r).astype(BF16)

        k_n = kv[:, lo:lo + LANES]
        ssk = jnp.sum(k_n * k_n, axis=-1, keepdims=True) + k_rope_ss
        rk = lax.rsqrt(ssk * inv_dim + RMS_EPS)
        k_ref[0, hh, :, 0:LANES] = (k_n * rk * gk_nope).astype(BF16)
        k_ref[0, hh, :, LANES:2 * LANES] = (k_rope_rot * rk).astype(BF16)
        v_ref[0, hh] = kv[:, lo + LANES:lo + 2 * LANES].astype(BF16)


def _mla_proj(x2d, gain, w_in, qn, kvn, wq, wqs, wkv, cos, sin, gq, gk, batch, seq):
    t, d = x2d.shape
    tm = MLA_TM
    tps = seq // tm
    hd = MLA_HEADS

    def full(arr):
        return pl.BlockSpec(arr.shape, lambda i: (0,) * arr.ndim)

    return pl.pallas_call(
        _mla_proj_kernel,
        grid=(t // tm,),
        in_specs=[
            pl.BlockSpec((tm, d), lambda i: (i, 0)),
            full(gain), full(w_in), full(qn), full(kvn), full(wq), full(wqs), full(wkv),
            pl.BlockSpec((tm, LANES), lambda i: (i % tps, 0)),
            pl.BlockSpec((tm, LANES), lambda i: (i % tps, 0)),
            full(gq), full(gk),
        ],
        out_specs=[
            pl.BlockSpec((1, hd, tm, MLA_HEAD_PAD), lambda i: (i // tps, 0, i % tps, 0)),
            pl.BlockSpec((1, hd, tm, MLA_HEAD_PAD), lambda i: (i // tps, 0, i % tps, 0)),
            pl.BlockSpec((1, hd, tm, MLA_V_DIM), lambda i: (i // tps, 0, i % tps, 0)),
        ],
        out_shape=[
            jax.ShapeDtypeStruct((batch, hd, seq, MLA_HEAD_PAD), BF16),
            jax.ShapeDtypeStruct((batch, hd, seq, MLA_HEAD_PAD), BF16),
            jax.ShapeDtypeStruct((batch, hd, seq, MLA_V_DIM), BF16),
        ],
        compiler_params=_params("parallel"),
        name="mla_proj",
    )(x2d, gain, w_in, qn, kvn, wq, wqs, wkv, cos, sin, gq, gk)


def _attn_kernel(q_ref, k_ref, v_ref, o_ref, *, n_q_blocks):
    tq = ATT_TQ
    row = lax.broadcasted_iota(jnp.int32, (tq, tq), 0) // CHUNK
    col = lax.broadcasted_iota(jnp.int32, (tq, tq), 1) // CHUNK
    visible = col <= row

    for qi in range(n_q_blocks):
        q0 = qi * tq
        q = q_ref[0, 0, q0:q0 + tq, :]
        s_diag = jnp.where(visible, _dot_nt(q, k_ref[0, 0, q0:q0 + tq, :]), MASK_VALUE)
        m = jnp.max(s_diag, axis=-1, keepdims=True)
        if qi > 0:
            s_off = _dot_nt(q, k_ref[0, 0, 0:q0, :])
            m = jnp.maximum(m, jnp.max(s_off, axis=-1, keepdims=True))
        p_diag = jnp.exp(s_diag - m)
        l = jnp.sum(p_diag, axis=-1, keepdims=True)
        acc = _dot(p_diag.astype(BF16), v_ref[0, 0, q0:q0 + tq, :])
        if qi > 0:
            p_off = jnp.exp(s_off - m)
            l = l + jnp.sum(p_off, axis=-1, keepdims=True)
            acc = acc + _dot(p_off.astype(BF16), v_ref[0, 0, 0:q0, :])
        o_ref[0, q0:q0 + tq, :] = (acc / l).astype(BF16)


def _attention(q, k, v):
    batch, hd, seq, _ = q.shape
    kern = functools.partial(_attn_kernel, n_q_blocks=seq // ATT_TQ)
    return pl.pallas_call(
        kern,
        grid=(batch, hd),
        in_specs=[
            pl.BlockSpec((1, 1, seq, MLA_HEAD_PAD), lambda b, h: (b, h, 0, 0)),
            pl.BlockSpec((1, 1, seq, MLA_HEAD_PAD), lambda b, h: (b, h, 0, 0)),
            pl.BlockSpec((1, 1, seq, MLA_V_DIM), lambda b, h: (b, h, 0, 0)),
        ],
        out_specs=pl.BlockSpec((1, seq, MLA_V_DIM), lambda b, h: (b, 0, h)),
        out_shape=jax.ShapeDtypeStruct((batch, seq, hd * MLA_V_DIM), BF16),
        compiler_params=_params("parallel", "parallel"),
        name="mla_attention",
    )(q, k, v)


def _rope_tables(seq, half):
    inv_freq = ROPE_THETA ** (-jnp.arange(half, dtype=F32) / half)
    ang = jnp.arange(seq).astype(F32)[:, None] * inv_freq[None, :]
    return jnp.cos(ang), jnp.sin(ang)


def _retention_decay_tables():
    sc = RET_SUPER
    log_gamma = jnp.log1p(-jnp.exp2(RET_GAMMA_BASE - jnp.arange(RET_HEADS, dtype=F32)))
    idx = jnp.arange(sc, dtype=F32)
    dist = jnp.abs(idx[:, None] - idx[None, :])
    chunk = jnp.arange(sc) // CHUNK
    visible = chunk[None, :] <= chunk[:, None]
    dmat = jnp.where(visible[None], jnp.exp(log_gamma[:, None, None] * dist[None]), 0.0)
    qd = jnp.exp(log_gamma[:, None] * (idx + 1.0))[:, :, None]
    kd = jnp.exp(log_gamma[:, None] * (sc - 1.0 - idx))[:, :, None]
    return dmat, qd, kd


def _pad_cols(w, width):
    return jnp.pad(w, ((0, 0), (0, width - w.shape[1])))


def _swap_halves(w):
    half = w.shape[-1] // 2
    return jnp.concatenate([w[..., half:], w[..., :half]], axis=-1)


def _mla_weights(w_in, w_qb, q_head_g, k_head_g):
    rq, rkv = MLA_Q_RANK, MLA_KV_RANK
    w_kr = w_in[:, rq + rkv:]
    w_in_l = jnp.concatenate(
        [w_in[:, :rq + rkv], _pad_cols(w_kr, LANES), _pad_cols(_swap_halves(w_kr), LANES)], axis=1)
    w_qb3 = w_qb.reshape(rq, MLA_HEADS, MLA_QK_DIM)
    nope, rope = w_qb3[..., :MLA_NOPE_DIM], w_qb3[..., MLA_NOPE_DIM:]
    zeros = jnp.zeros((rq, MLA_HEADS, LANES - MLA_ROPE_DIM), w_qb.dtype)
    wq = jnp.concatenate([nope, rope, zeros], axis=-1).reshape(rq, MLA_HEADS * MLA_HEAD_PAD)
    wqs = jnp.concatenate([_swap_halves(rope), zeros], axis=-1).reshape(rq, MLA_HEADS * LANES)

    def gains(g):
        rope_g = g[MLA_NOPE_DIM:]
        pad = jnp.zeros((LANES - MLA_ROPE_DIM,), g.dtype)
        return jnp.stack([g[:MLA_NOPE_DIM], jnp.concatenate([rope_g, pad]),
                          jnp.concatenate([_swap_halves(rope_g), pad])])

    return w_in_l, wq, wqs, gains(q_head_g), gains(k_head_g)


def kernel(x, ret_norm, ret_w_in, ret_gn, ret_w_out, mla_norm, mla_w_in, mla_q_norm, mla_w_qb, mla_kv_norm,
           mla_w_kvb, mla_q_head_norm, mla_k_head_norm, mla_w_out, ffn_norm, ffn_w_in, ffn_conv_w, ffn_conv_b,
           ffn_w_out):
    batch, seq, d = x.shape
    x2d = x.reshape(batch * seq, d)

    cos_r, sin_r = _rope_tables(seq, RET_QK_DIM // 2)
    dmat, qd, kd = _retention_decay_tables()
    qkvg = _ret_in_proj(x2d, ret_norm[0][None, :], ret_w_in[0].astype(BF16), cos_r, sin_r, seq)
    ret = _ret_core(qkvg, dmat, qd, kd, ret_gn[0][:, None, :], batch, seq)
    x2d = _out_proj(ret.reshape(batch * seq, -1), ret_w_out[0].astype(BF16), x2d)
    x2d = _ffn(x2d, ffn_norm[0][None, :], ffn_w_in[0].astype(BF16), ffn_conv_w[0], ffn_conv_b[0][None, :],
               ffn_w_out[0].astype(BF16), seq)

    cos_m, sin_m = _rope_tables(seq, MLA_ROPE_DIM // 2)
    pad = jnp.zeros((seq, LANES - MLA_ROPE_DIM), F32)
    cos_t = jnp.concatenate([cos_m, cos_m, pad], axis=1)
    sin_t = jnp.concatenate([-sin_m, sin_m, pad], axis=1)
    w_in_l, wq, wqs, gq, gk = _mla_weights(mla_w_in[0], mla_w_qb[0], mla_q_head_norm[0], mla_k_head_norm[0])
    q, k, v = _mla_proj(x2d, mla_norm[0][None, :], w_in_l.astype(BF16), mla_q_norm[0][None, :],
                        mla_kv_norm[0][None, :], wq.astype(BF16), wqs.astype(BF16), mla_w_kvb[0].astype(BF16),
                        cos_t, sin_t, gq, gk, batch, seq)
    att = _attention(q, k, v)
    x2d = _out_proj(att.reshape(batch * seq, -1), mla_w_out[0].astype(BF16), x2d)
    x2d = _ffn(x2d, ffn_norm[1][None, :], ffn_w_in[1].astype(BF16), ffn_conv_w[1], ffn_conv_b[1][None, :],
               ffn_w_out[1].astype(BF16), seq)
    return x2d.reshape(batch, seq, d)
```

```python
import functools

import jax
import jax.numpy as jnp
from jax import lax
from jax.experimental import pallas as pl
from jax.experimental.pallas import tpu as pltpu

F32 = jnp.float32
BF16 = jnp.bfloat16

CHUNK = 64
RMS_EPS = 1e-6
ROPE_THETA = 10000.0
RET_HEADS = 4
RET_QK_DIM = 256
RET_V_DIM = 512
RET_GAMMA_BASE = -5.0
MLA_HEADS = 8
MLA_Q_RANK = 384
MLA_KV_RANK = 256
MLA_NOPE_DIM = 128
MLA_ROPE_DIM = 64
MLA_V_DIM = 128
MLA_QK_DIM = MLA_NOPE_DIM + MLA_ROPE_DIM
MLA_HEAD_PAD = 256
MASK_VALUE = -1e30
CONV_WIDTH = 3

LANES = 128
SUBLANES = 8
VMEM_LIMIT_BYTES = 52 * 1024 * 1024

RET_SUPER = 256
PROJ_TM = 1024
PROJ_TN = 1024
RET_HEADS_PER_STEP = 2
FFN_TM = 512
FFN_CHUNK = 256
MLA_TM = 512
ATT_TQ = 256


def _params(*sem):
    return pltpu.CompilerParams(dimension_semantics=sem, vmem_limit_bytes=VMEM_LIMIT_BYTES)


def _rms_scale(x):
    return lax.rsqrt(jnp.mean(x * x, axis=-1, keepdims=True) + RMS_EPS)


def _sigmoid(x):
    return 1.0 / (1.0 + jnp.exp(-x))


def _dot(a, b):
    return jnp.dot(a, b, preferred_element_type=F32)


def _dot_nt(a, b):
    return lax.dot_general(a, b, (((1,), (1,)), ((), ())), preferred_element_type=F32)


def _dot_tn(a, b):
    return lax.dot_general(a, b, (((0,), (0,)), ((), ())), preferred_element_type=F32)


def _ret_in_kernel(x_ref, gain_ref, w_ref, cos_ref, sin_ref, o_ref, h_ref, *, n_q_tiles, n_rope_tiles):
    j = pl.program_id(1)
    half = RET_QK_DIM // 2
    n_heads = w_ref.shape[1] // RET_QK_DIM

    @pl.when(j == 0)
    def _():
        x = x_ref[...]
        h_ref[...] = (x * _rms_scale(x) * gain_ref[...]).astype(BF16)

    @pl.when(j < n_rope_tiles)
    def _():
        scale = jnp.where(j >= n_q_tiles, RET_QK_DIM ** -0.5, 1.0).astype(F32)
        cos = cos_ref[...] * scale
        sin = sin_ref[...] * scale
        for hh in range(n_heads):
            lo = hh * RET_QK_DIM
            acc = _dot(h_ref[...], w_ref[:, lo:lo + RET_QK_DIM])
            x1 = acc[:, :half]
            x2 = acc[:, half:]
            o_ref[:, lo:lo + half] = (x1 * cos - x2 * sin).astype(BF16)
            o_ref[:, lo + half:lo + RET_QK_DIM] = (x2 * cos + x1 * sin).astype(BF16)

    @pl.when(j >= n_rope_tiles)
    def _():
        for hh in range(n_heads):
            lo = hh * RET_QK_DIM
            o_ref[:, lo:lo + RET_QK_DIM] = _dot(h_ref[...], w_ref[:, lo:lo + RET_QK_DIM]).astype(BF16)


def _ret_in_proj(x2d, gain, w, cos, sin, seq):
    t, d = x2d.shape
    n = w.shape[1]
    tm, tn = PROJ_TM, PROJ_TN
    tps = seq // tm
    qk_cols = RET_HEADS * RET_QK_DIM
    kern = functools.partial(_ret_in_kernel, n_q_tiles=qk_cols // tn, n_rope_tiles=2 * qk_cols // tn)
    return pl.pallas_call(
        kern,
        grid=(t // tm, n // tn),
        in_specs=[
            pl.BlockSpec((tm, d), lambda i, j: (i, 0)),
            pl.BlockSpec((1, d), lambda i, j: (0, 0)),
            pl.BlockSpec((d, tn), lambda i, j: (0, j)),
            pl.BlockSpec((tm, RET_QK_DIM // 2), lambda i, j: (i % tps, 0)),
            pl.BlockSpec((tm, RET_QK_DIM // 2), lambda i, j: (i % tps, 0)),
        ],
        out_specs=pl.BlockSpec((tm, tn), lambda i, j: (i, j)),
        out_shape=jax.ShapeDtypeStruct((t, n), BF16),
        scratch_shapes=[pltpu.VMEM((tm, d), BF16)],
        compiler_params=_params("parallel", "arbitrary"),
        name="ret_in_proj",
    )(x2d, gain, w, cos, sin)


def _ret_core_kernel(q_ref, k_ref, v_ref, g_ref, dmat_ref, qd_ref, kd_ref, gn_ref, o_ref, state_ref, *, n_steps):
    dk, dv = RET_QK_DIM, RET_V_DIM
    state_ref[...] = jnp.zeros_like(state_ref)

    for sc in range(n_steps):
        rows = slice(sc * RET_SUPER, (sc + 1) * RET_SUPER)
        for hh in range(RET_HEADS_PER_STEP):
            qd = qd_ref[hh]
            kd = kd_ref[hh]
            step_decay = qd[RET_SUPER - 1:RET_SUPER, :]
            q = q_ref[0, rows, hh * dk:(hh + 1) * dk]
            k = k_ref[0, rows, hh * dk:(hh + 1) * dk]
            v = v_ref[0, rows, hh * dv:(hh + 1) * dv]
            scores = _dot_nt(q, k) * dmat_ref[hh]
            inner = _dot(scores.astype(BF16), v)
            state = state_ref[hh]
            q_scaled = (q.astype(F32) * qd).astype(BF16)
            cross = _dot(q_scaled, state.astype(BF16))
            k_scaled = (k.astype(F32) * kd).astype(BF16)
            state_ref[hh] = state * step_decay + _dot_tn(k_scaled, v)
            out = inner + cross
            out = out * _rms_scale(out) * gn_ref[hh]
            g = g_ref[0, rows, hh * dv:(hh + 1) * dv].astype(F32)
            o_ref[0, rows, hh * dv:(hh + 1) * dv] = (out * (g * _sigmoid(g))).astype(BF16)


def _ret_core(qkvg, dmat, qd, kd, gn, batch, seq):
    hd, dk, dv = RET_HEADS, RET_QK_DIM, RET_V_DIM
    hps = RET_HEADS_PER_STEP
    groups = hd // hps
    qkvg3 = qkvg.reshape(batch, seq, qkvg.shape[-1])
    k_blk0 = groups
    v_blk0 = 2 * hd * dk // (hps * dv)
    g_blk0 = v_blk0 + groups
    kern = functools.partial(_ret_core_kernel, n_steps=seq // RET_SUPER)
    return pl.pallas_call(
        kern,
        grid=(batch, groups),
        in_specs=[
            pl.BlockSpec((1, seq, hps * dk), lambda b, h: (b, 0, h)),
            pl.BlockSpec((1, seq, hps * dk), lambda b, h: (b, 0, k_blk0 + h)),
            pl.BlockSpec((1, seq, hps * dv), lambda b, h: (b, 0, v_blk0 + h)),
            pl.BlockSpec((1, seq, hps * dv), lambda b, h: (b, 0, g_blk0 + h)),
            pl.BlockSpec((hps, RET_SUPER, RET_SUPER), lambda b, h: (h, 0, 0)),
            pl.BlockSpec((hps, RET_SUPER, 1), lambda b, h: (h, 0, 0)),
            pl.BlockSpec((hps, RET_SUPER, 1), lambda b, h: (h, 0, 0)),
            pl.BlockSpec((hps, 1, dv), lambda b, h: (h, 0, 0)),
        ],
        out_specs=pl.BlockSpec((1, seq, hps * dv), lambda b, h: (b, 0, h)),
        out_shape=jax.ShapeDtypeStruct((batch, seq, hd * dv), BF16),
        scratch_shapes=[pltpu.VMEM((hps, dk, dv), F32)],
        compiler_params=_params("parallel", "parallel"),
        name="ret_core",
    )(qkvg3, qkvg3, qkvg3, qkvg3, dmat, qd, kd, gn)


def _mix_ffn_kernel(a_ref, wo_ref, res_ref, gain_ref, w_in_ref, cw_ref, cb_ref, w_out_ref, o_ref,
                    h_ref, act_ref, carry_ref, *, tiles_per_seq):
    i = pl.program_id(0)
    tm = res_ref.shape[0]
    ffn_dim = w_out_ref.shape[0]

    x1 = res_ref[...] + _dot(a_ref[...], wo_ref[...])
    o_ref[...] = x1
    h_ref[...] = (x1 * _rms_scale(x1) * gain_ref[...]).astype(BF16)

    seq_start = lax.rem(i, tiles_per_seq) == 0
    row = lax.broadcasted_iota(jnp.int32, (tm, 1), 0)
    for c in range(ffn_dim // FFN_CHUNK):
        lo = c * FFN_CHUNK
        a = _dot(h_ref[...], w_in_ref[:, lo:lo + FFN_CHUNK])
        g = _dot(h_ref[...], w_in_ref[:, ffn_dim + lo:ffn_dim + lo + FFN_CHUNK])
        prev = jnp.where(seq_start, 0.0, carry_ref[:, lo:lo + FFN_CHUNK])
        prev1 = prev[SUBLANES - 1:SUBLANES, :]
        prev2 = prev[SUBLANES - 2:SUBLANES - 1, :]
        g1 = jnp.where(row == 0, prev1, pltpu.roll(g, 1, 0))
        g2 = jnp.where(row == 0, prev2, jnp.where(row == 1, prev1, pltpu.roll(g, 2, 0)))
        carry_ref[:, lo:lo + FFN_CHUNK] = g[tm - SUBLANES:tm, :]
        cw = cw_ref[:, lo:lo + FFN_CHUNK]
        gc = g2 * cw[0:1, :] + g1 * cw[1:2, :] + g * cw[2:3, :] + cb_ref[:, lo:lo + FFN_CHUNK]
        act_ref[:, lo:lo + FFN_CHUNK] = (gc * _sigmoid(gc) * a).astype(BF16)

    o_ref[...] += _dot(act_ref[...], w_out_ref[...])


def _resident(arr):
    return pl.BlockSpec(arr.shape, lambda i: (0,) * arr.ndim, pipeline_mode=pl.Buffered(1))


def _mix_ffn(a, w_o, res, gain, w_in, conv_w, conv_b, w_out, seq):
    t, d = res.shape
    ka = a.shape[1]
    f = w_out.shape[0]
    tm = FFN_TM
    kern = functools.partial(_mix_ffn_kernel, tiles_per_seq=seq // tm)
    return pl.pallas_call(
        kern,
        grid=(t // tm,),
        in_specs=[
            pl.BlockSpec((tm, ka), lambda i: (i, 0)),
            _resident(w_o),
            pl.BlockSpec((tm, d), lambda i: (i, 0)),
            _resident(gain), _resident(w_in), _resident(conv_w), _resident(conv_b), _resident(w_out),
        ],
        out_specs=pl.BlockSpec((tm, d), lambda i: (i, 0)),
        out_shape=jax.ShapeDtypeStruct((t, d), F32),
        scratch_shapes=[pltpu.VMEM((tm, d), BF16), pltpu.VMEM((tm, f), BF16), pltpu.VMEM((SUBLANES, f), F32)],
        compiler_params=_params("arbitrary"),
        name="mix_ffn",
    )(a, w_o, res, gain, w_in, conv_w, conv_b, w_out)


def _mla_proj_kernel(x_ref, gain_ref, w_in_ref, qn_ref, kvn_ref, wq_ref, wqs_ref, wkv_ref,
                     c_ref, s_ref, gq_ref, gk_ref, q_ref, k_ref, v_ref):
    x = x_ref[...]
    h = (x * _rms_scale(x) * gain_ref[...]).astype(BF16)
    p = _dot(h, w_in_ref[...])
    c_q = p[:, :MLA_Q_RANK]
    c_kv = p[:, MLA_Q_RANK:MLA_Q_RANK + MLA_KV_RANK]
    k_r = p[:, MLA_Q_RANK + MLA_KV_RANK:MLA_Q_RANK + MLA_KV_RANK + LANES]
    k_rs = p[:, MLA_Q_RANK + MLA_KV_RANK + LANES:]
    c_q = (c_q * _rms_scale(c_q) * qn_ref[...]).astype(BF16)
    c_kv = (c_kv * _rms_scale(c_kv) * kvn_ref[...]).astype(BF16)
    q_all = _dot(c_q, wq_ref[...])
    q_swp = _dot(c_q, wqs_ref[...])
    kv = _dot(c_kv, wkv_ref[...])

    cos = c_ref[...]
    sin = s_ref[...]
    gq_nope, gq_rope, gq_swap = gq_ref[0:1, :], gq_ref[1:2, :], gq_ref[2:3, :]
    gk_nope, gk_rope, gk_swap = gk_ref[0:1, :], gk_ref[1:2, :], gk_ref[2:3, :]
    inv_dim = 1.0 / MLA_QK_DIM
    q_scale = MLA_QK_DIM ** -0.5

    k_rope_ss = jnp.sum(k_r * k_r, axis=-1, keepdims=True)
    k_rope_rot = k_r * gk_rope * cos + k_rs * gk_swap * sin

    for hh in range(MLA_HEADS):
        lo = hh * MLA_HEAD_PAD
        q_n = q_all[:, lo:lo + LANES]
        q_r = q_all[:, lo + LANES:lo + 2 * LANES]
        q_s = q_swp[:, hh * LANES:(hh + 1) * LANES]
        ss = jnp.sum(q_n * q_n, axis=-1, keepdims=True) + jnp.sum(q_r * q_r, axis=-1, keepdims=True)
        r = lax.rsqrt(ss * inv_dim + RMS_EPS) * q_scale
        q_ref[0, hh, :, 0:LANES] = (q_n * r * gq_nope).astype(BF16)
        q_ref[0, hh, :, LANES:2 * LANES] = ((q_r * gq_rope * cos + q_s * gq_swap * sin) * r).astype(BF16)

        k_n = kv[:, lo:lo + LANES]
        ssk = jnp.sum(k_n * k_n, axis=-1, keepdims=True) + k_rope_ss
        rk = lax.rsqrt(ssk * inv_dim + RMS_EPS)
        k_ref[0, hh, :, 0:LANES] = (k_n * rk * gk_nope).astype(BF16)
        k_ref[0, hh, :, LANES:2 * LANES] = (k_rope_rot * rk).astype(BF16)
        v_ref[0, hh] = kv[:, lo + LANES:lo + 2 * LANES].astype(BF16)


def _mla_proj(x2d, gain, w_in, qn, kvn, wq, wqs, wkv, cos, sin, gq, gk, batch, seq):
    t, d = x2d.shape
    tm = MLA_TM
    tps = seq // tm
    hd = MLA_HEADS

    def full(arr):
        return pl.BlockSpec(arr.shape, lambda i: (0,) * arr.ndim)

    return pl.pallas_call(
        _mla_proj_kernel,
        grid=(t // tm,),
        in_specs=[
            pl.BlockSpec((tm, d), lambda i: (i, 0)),
            full(gain), full(w_in), full(qn), full(kvn), full(wq), full(wqs), full(wkv),
            pl.BlockSpec((tm, LANES), lambda i: (i % tps, 0)),
            pl.BlockSpec((tm, LANES), lambda i: (i % tps, 0)),
            full(gq), full(gk),
        ],
        out_specs=[
            pl.BlockSpec((1, hd, tm, MLA_HEAD_PAD), lambda i: (i // tps, 0, i % tps, 0)),
            pl.BlockSpec((1, hd, tm, MLA_HEAD_PAD), lambda i: (i // tps, 0, i % tps, 0)),
            pl.BlockSpec((1, hd, tm, MLA_V_DIM), lambda i: (i // tps, 0, i % tps, 0)),
        ],
        out_shape=[
            jax.ShapeDtypeStruct((batch, hd, seq, MLA_HEAD_PAD), BF16),
            jax.ShapeDtypeStruct((batch, hd, seq, MLA_HEAD_PAD), BF16),
            jax.ShapeDtypeStruct((batch, hd, seq, MLA_V_DIM), BF16),
        ],
        compiler_params=_params("parallel"),
        name="mla_proj",
    )(x2d, gain, w_in, qn, kvn, wq, wqs, wkv, cos, sin, gq, gk)


def _attn_kernel(q_ref, k_ref, v_ref, o_ref, *, n_q_blocks):
    tq = ATT_TQ
    row = lax.broadcasted_iota(jnp.int32, (tq, tq), 0) // CHUNK
    col = lax.broadcasted_iota(jnp.int32, (tq, tq), 1) // CHUNK
    visible = col <= row

    for qi in range(n_q_blocks):
        q0 = qi * tq
        q = q_ref[0, 0, q0:q0 + tq, :]
        s_diag = jnp.where(visible, _dot_nt(q, k_ref[0, 0, q0:q0 + tq, :]), MASK_VALUE)
        m = jnp.max(s_diag, axis=-1, keepdims=True)
        if qi > 0:
            s_off = _dot_nt(q, k_ref[0, 0, 0:q0, :])
            m = jnp.maximum(m, jnp.max(s_off, axis=-1, keepdims=True))
        p_diag = jnp.exp(s_diag - m)
        l = jnp.sum(p_diag, axis=-1, keepdims=True)
        acc = _dot(p_diag.astype(BF16), v_ref[0, 0, q0:q0 + tq, :])
        if qi > 0:
            p_off = jnp.exp(s_off - m)
            l = l + jnp.sum(p_off, axis=-1, keepdims=True)
            acc = acc + _dot(p_off.astype(BF16), v_ref[0, 0, 0:q0, :])
        o_ref[0, q0:q0 + tq, :] = (acc / l).astype(BF16)


def _attention(q, k, v):
    batch, hd, seq, _ = q.shape
    kern = functools.partial(_attn_kernel, n_q_blocks=seq // ATT_TQ)
    return pl.pallas_call(
        kern,
        grid=(batch, hd),
        in_specs=[
            pl.BlockSpec((1, 1, seq, MLA_HEAD_PAD), lambda b, h: (b, h, 0, 0)),
            pl.BlockSpec((1, 1, seq, MLA_HEAD_PAD), lambda b, h: (b, h, 0, 0)),
            pl.BlockSpec((1, 1, seq, MLA_V_DIM), lambda b, h: (b, h, 0, 0)),
        ],
        out_specs=pl.BlockSpec((1, seq, MLA_V_DIM), lambda b, h: (b, 0, h)),
        out_shape=jax.ShapeDtypeStruct((batch, seq, hd * MLA_V_DIM), BF16),
        compiler_params=_params("parallel", "parallel"),
        name="mla_attention",
    )(q, k, v)


def _rope_tables(seq, half):
    inv_freq = ROPE_THETA ** (-jnp.arange(half, dtype=F32) / half)
    ang = jnp.arange(seq).astype(F32)[:, None] * inv_freq[None, :]
    return jnp.cos(ang), jnp.sin(ang)


def _retention_decay_tables():
    sc = RET_SUPER
    log_gamma = jnp.log1p(-jnp.exp2(RET_GAMMA_BASE - jnp.arange(RET_HEADS, dtype=F32)))
    idx = jnp.arange(sc, dtype=F32)
    dist = jnp.abs(idx[:, None] - idx[None, :])
    chunk = jnp.arange(sc) // CHUNK
    visible = chunk[None, :] <= chunk[:, None]
    dmat = jnp.where(visible[None], jnp.exp(log_gamma[:, None, None] * dist[None]), 0.0)
    qd = jnp.exp(log_gamma[:, None] * (idx + 1.0))[:, :, None]
    kd = jnp.exp(log_gamma[:, None] * (sc - 1.0 - idx))[:, :, None]
    return dmat, qd, kd


def _pad_cols(w, width):
    return jnp.pad(w, ((0, 0), (0, width - w.shape[1])))


def _swap_halves(w):
    half = w.shape[-1] // 2
    return jnp.concatenate([w[..., half:], w[..., :half]], axis=-1)


def _mla_weights(w_in, w_qb, q_head_g, k_head_g):
    rq, rkv = MLA_Q_RANK, MLA_KV_RANK
    w_kr = w_in[:, rq + rkv:]
    w_in_l = jnp.concatenate(
        [w_in[:, :rq + rkv], _pad_cols(w_kr, LANES), _pad_cols(_swap_halves(w_kr), LANES)], axis=1)
    w_qb3 = w_qb.reshape(rq, MLA_HEADS, MLA_QK_DIM)
    nope, rope = w_qb3[..., :MLA_NOPE_DIM], w_qb3[..., MLA_NOPE_DIM:]
    zeros = jnp.zeros((rq, MLA_HEADS, LANES - MLA_ROPE_DIM), w_qb.dtype)
    wq = jnp.concatenate([nope, rope, zeros], axis=-1).reshape(rq, MLA_HEADS * MLA_HEAD_PAD)
    wqs = jnp.concatenate([_swap_halves(rope), zeros], axis=-1).reshape(rq, MLA_HEADS * LANES)

    def gains(g):
        rope_g = g[MLA_NOPE_DIM:]
        pad = jnp.zeros((LANES - MLA_ROPE_DIM,), g.dtype)
        return jnp.stack([g[:MLA_NOPE_DIM], jnp.concatenate([rope_g, pad]),
                          jnp.concatenate([_swap_halves(rope_g), pad])])

    return w_in_l, wq, wqs, gains(q_head_g), gains(k_head_g)


def kernel(x, ret_norm, ret_w_in, ret_gn, ret_w_out, mla_norm, mla_w_in, mla_q_norm, mla_w_qb, mla_kv_norm,
           mla_w_kvb, mla_q_head_norm, mla_k_head_norm, mla_w_out, ffn_norm, ffn_w_in, ffn_conv_w, ffn_conv_b,
           ffn_w_out):
    batch, seq, d = x.shape
    x2d = x.reshape(batch * seq, d)

    cos_r, sin_r = _rope_tables(seq, RET_QK_DIM // 2)
    dmat, qd, kd = _retention_decay_tables()
    qkvg = _ret_in_proj(x2d, ret_norm[0][None, :], ret_w_in[0].astype(BF16), cos_r, sin_r, seq)
    ret = _ret_core(qkvg, dmat, qd, kd, ret_gn[0][:, None, :], batch, seq)
    x2d = _mix_ffn(ret.reshape(batch * seq, -1), ret_w_out[0].astype(BF16), x2d, ffn_norm[0][None, :],
                   ffn_w_in[0].astype(BF16), ffn_conv_w[0], ffn_conv_b[0][None, :], ffn_w_out[0].astype(BF16), seq)

    cos_m, sin_m = _rope_tables(seq, MLA_ROPE_DIM // 2)
    pad = jnp.zeros((seq, LANES - MLA_ROPE_DIM), F32)
    cos_t = jnp.concatenate([cos_m, cos_m, pad], axis=1)
    sin_t = jnp.concatenate([-sin_m, sin_m, pad], axis=1)
    w_in_l, wq, wqs, gq, gk = _mla_weights(mla_w_in[0], mla_w_qb[0], mla_q_head_norm[0], mla_k_head_norm[0])
    q, k, v = _mla_proj(x2d, mla_norm[0][None, :], w_in_l.astype(BF16), mla_q_norm[0][None, :],
                        mla_kv_norm[0][None, :], wq.astype(BF16), wqs.astype(BF16), mla_w_kvb[0].astype(BF16),
                        cos_t, sin_t, gq, gk, batch, seq)
    att = _attention(q, k, v)
    x2d = _mix_ffn(att.reshape(batch * seq, -1), mla_w_out[0].astype(BF16), x2d, ffn_norm[1][None, :],
                   ffn_w_in[1].astype(BF16), ffn_conv_w[1], ffn_conv_b[1][None, :], ffn_w_out[1].astype(BF16), seq)
    return x2d.reshape(batch, seq, d)
```

```python
import functools

import jax
import jax.numpy as jnp
from jax import lax
from jax.experimental import pallas as pl
from jax.experimental.pallas import tpu as pltpu

F32 = jnp.float32
BF16 = jnp.bfloat16

CHUNK = 64
RMS_EPS = 1e-6
ROPE_THETA = 10000.0
RET_HEADS = 4
RET_QK_DIM = 256
RET_V_DIM = 512
RET_GAMMA_BASE = -5.0
MLA_HEADS = 8
MLA_Q_RANK = 384
MLA_KV_RANK = 256
MLA_NOPE_DIM = 128
MLA_ROPE_DIM = 64
MLA_V_DIM = 128
MLA_QK_DIM = MLA_NOPE_DIM + MLA_ROPE_DIM
MLA_HEAD_PAD = 256
MLA_Q_COLS = 3 * 128
MASK_VALUE = -1e30
CONV_WIDTH = 3
LOG2_E = 1.4426950408889634

LANES = 128
SUBLANES = 8
VMEM_LIMIT_BYTES = 52 * 1024 * 1024

RET_SUPER = 256
PROJ_TM = 512
RET_HEADS_PER_STEP = 2
FFN_TM = 512
FFN_CHUNK = 256
MLA_TM = 512
ATT_TQ = 256


def _params(*sem):
    return pltpu.CompilerParams(dimension_semantics=sem, vmem_limit_bytes=VMEM_LIMIT_BYTES)


def _resident(arr):
    return pl.BlockSpec(arr.shape, lambda i: (0,) * arr.ndim, pipeline_mode=pl.Buffered(1))


def _rms_scale(x):
    return lax.rsqrt(jnp.mean(x * x, axis=-1, keepdims=True) + RMS_EPS)


def _sigmoid(x):
    return 1.0 / (1.0 + jnp.exp(-x))


def _dot(a, b):
    return jnp.dot(a, b, preferred_element_type=F32)


def _dot_nt(a, b):
    return lax.dot_general(a, b, (((1,), (1,)), ((), ())), preferred_element_type=F32)


def _dot_tn(a, b):
    return lax.dot_general(a, b, (((0,), (0,)), ((), ())), preferred_element_type=F32)


def _ret_in_kernel(x_ref, gain_ref, w_ref, cos_ref, sin_ref, o_ref, h_ref):
    half = RET_QK_DIM // 2
    x = x_ref[...]
    h_ref[...] = (x * _rms_scale(x) * gain_ref[...]).astype(BF16)
    cos = cos_ref[...]
    sin = sin_ref[...]
    k_scale = RET_QK_DIM ** -0.5
    cos_k = cos * k_scale
    sin_k = sin * k_scale
    for c in range(w_ref.shape[1] // RET_QK_DIM):
        lo = c * RET_QK_DIM
        acc = _dot(h_ref[...], w_ref[:, lo:lo + RET_QK_DIM])
        if c < 2 * RET_HEADS:
            cs, sn = (cos, sin) if c < RET_HEADS else (cos_k, sin_k)
            x1 = acc[:, :half]
            x2 = acc[:, half:]
            o_ref[:, lo:lo + half] = (x1 * cs - x2 * sn).astype(BF16)
            o_ref[:, lo + half:lo + RET_QK_DIM] = (x2 * cs + x1 * sn).astype(BF16)
        else:
            o_ref[:, lo:lo + RET_QK_DIM] = acc.astype(BF16)


def _ret_in_proj(x2d, gain, w, cos, sin, seq):
    t, d = x2d.shape
    n = w.shape[1]
    tm = PROJ_TM
    tps = seq // tm
    return pl.pallas_call(
        _ret_in_kernel,
        grid=(t // tm,),
        in_specs=[
            pl.BlockSpec((tm, d), lambda i: (i, 0)),
            _resident(gain), _resident(w),
            pl.BlockSpec((tm, RET_QK_DIM // 2), lambda i: (i % tps, 0)),
            pl.BlockSpec((tm, RET_QK_DIM // 2), lambda i: (i % tps, 0)),
        ],
        out_specs=pl.BlockSpec((tm, n), lambda i: (i, 0)),
        out_shape=jax.ShapeDtypeStruct((t, n), BF16),
        scratch_shapes=[pltpu.VMEM((tm, d), BF16)],
        compiler_params=_params("parallel"),
        name="ret_in_proj",
    )(x2d, gain, w, cos, sin)


def _ret_core_kernel(q_ref, k_ref, v_ref, g_ref, dmat_ref, qd_ref, kd_ref, gn_ref, o_ref, state_ref, *, n_steps):
    dk, dv = RET_QK_DIM, RET_V_DIM
    state_ref[...] = jnp.zeros_like(state_ref)

    for sc in range(n_steps):
        rows = slice(sc * RET_SUPER, (sc + 1) * RET_SUPER)
        for hh in range(RET_HEADS_PER_STEP):
            qd = qd_ref[hh]
            kd = kd_ref[hh]
            step_decay = qd[RET_SUPER - 1:RET_SUPER, :]
            q = q_ref[0, rows, hh * dk:(hh + 1) * dk]
            k = k_ref[0, rows, hh * dk:(hh + 1) * dk]
            v = v_ref[0, rows, hh * dv:(hh + 1) * dv]
            scores = _dot_nt(q, k) * dmat_ref[hh]
            inner = _dot(scores.astype(BF16), v)
            state = state_ref[hh]
            q_scaled = (q.astype(F32) * qd).astype(BF16)
            cross = _dot(q_scaled, state.astype(BF16))
            k_scaled = (k.astype(F32) * kd).astype(BF16)
            state_ref[hh] = state * step_decay + _dot_tn(k_scaled, v)
            out = inner + cross
            out = out * _rms_scale(out) * gn_ref[hh]
            g = g_ref[0, rows, hh * dv:(hh + 1) * dv].astype(F32)
            o_ref[0, rows, hh * dv:(hh + 1) * dv] = (out * (g * _sigmoid(g))).astype(BF16)


def _ret_core(qkvg, dmat, qd, kd, gn, batch, seq):
    hd, dk, dv = RET_HEADS, RET_QK_DIM, RET_V_DIM
    hps = RET_HEADS_PER_STEP
    groups = hd // hps
    qkvg3 = qkvg.reshape(batch, seq, qkvg.shape[-1])
    k_blk0 = groups
    v_blk0 = 2 * hd * dk // (hps * dv)
    g_blk0 = v_blk0 + groups
    kern = functools.partial(_ret_core_kernel, n_steps=seq // RET_SUPER)
    return pl.pallas_call(
        kern,
        grid=(batch, groups),
        in_specs=[
            pl.BlockSpec((1, seq, hps * dk), lambda b, h: (b, 0, h)),
            pl.BlockSpec((1, seq, hps * dk), lambda b, h: (b, 0, k_blk0 + h)),
            pl.BlockSpec((1, seq, hps * dv), lambda b, h: (b, 0, v_blk0 + h)),
            pl.BlockSpec((1, seq, hps * dv), lambda b, h: (b, 0, g_blk0 + h)),
            pl.BlockSpec((hps, RET_SUPER, RET_SUPER), lambda b, h: (h, 0, 0)),
            pl.BlockSpec((hps, RET_SUPER, 1), lambda b, h: (h, 0, 0)),
            pl.BlockSpec((hps, RET_SUPER, 1), lambda b, h: (h, 0, 0)),
            pl.BlockSpec((hps, 1, dv), lambda b, h: (h, 0, 0)),
        ],
        out_specs=pl.BlockSpec((1, seq, hps * dv), lambda b, h: (b, 0, h)),
        out_shape=jax.ShapeDtypeStruct((batch, seq, hd * dv), BF16),
        scratch_shapes=[pltpu.VMEM((hps, dk, dv), F32)],
        compiler_params=_params("parallel", "parallel"),
        name="ret_core",
    )(qkvg3, qkvg3, qkvg3, qkvg3, dmat, qd, kd, gn)


def _mix_ffn_kernel(a_ref, wo_ref, res_ref, gain_ref, w_in_ref, cw_ref, cb_ref, w_out_ref, o_ref,
                    h_ref, act_ref, carry_ref, *, tiles_per_seq):
    i = pl.program_id(0)
    tm = res_ref.shape[0]
    ffn_dim = w_out_ref.shape[0]

    x1 = res_ref[...] + _dot(a_ref[...], wo_ref[...])
    o_ref[...] = x1
    h_ref[...] = (x1 * _rms_scale(x1) * gain_ref[...]).astype(BF16)

    seq_start = lax.rem(i, tiles_per_seq) == 0
    row = lax.broadcasted_iota(jnp.int32, (tm, 1), 0)
    for c in range(ffn_dim // FFN_CHUNK):
        lo = c * FFN_CHUNK
        a = _dot(h_ref[...], w_in_ref[:, lo:lo + FFN_CHUNK])
        g = _dot(h_ref[...], w_in_ref[:, ffn_dim + lo:ffn_dim + lo + FFN_CHUNK])
        prev = jnp.where(seq_start, 0.0, carry_ref[:, lo:lo + FFN_CHUNK])
        prev1 = prev[SUBLANES - 1:SUBLANES, :]
        prev2 = prev[SUBLANES - 2:SUBLANES - 1, :]
        g1 = jnp.where(row == 0, prev1, pltpu.roll(g, 1, 0))
        g2 = jnp.where(row == 0, prev2, jnp.where(row == 1, prev1, pltpu.roll(g, 2, 0)))
        carry_ref[:, lo:lo + FFN_CHUNK] = g[tm - SUBLANES:tm, :]
        cw = cw_ref[:, lo:lo + FFN_CHUNK]
        gc = g2 * cw[0:1, :] + g1 * cw[1:2, :] + g * cw[2:3, :] + cb_ref[:, lo:lo + FFN_CHUNK]
        act_ref[:, lo:lo + FFN_CHUNK] = (gc * _sigmoid(gc) * a).astype(BF16)

    o_ref[...] += _dot(act_ref[...], w_out_ref[...])


def _mix_ffn(a, w_o, res, gain, w_in, conv_w, conv_b, w_out, seq):
    t, d = res.shape
    ka = a.shape[1]
    f = w_out.shape[0]
    tm = FFN_TM
    kern = functools.partial(_mix_ffn_kernel, tiles_per_seq=seq // tm)
    return pl.pallas_call(
        kern,
        grid=(t // tm,),
        in_specs=[
            pl.BlockSpec((tm, ka), lambda i: (i, 0)),
            _resident(w_o),
            pl.BlockSpec((tm, d), lambda i: (i, 0)),
            _resident(gain), _resident(w_in), _resident(conv_w), _resident(conv_b), _resident(w_out),
        ],
        out_specs=pl.BlockSpec((tm, d), lambda i: (i, 0)),
        out_shape=jax.ShapeDtypeStruct((t, d), F32),
        scratch_shapes=[pltpu.VMEM((tm, d), BF16), pltpu.VMEM((tm, f), BF16), pltpu.VMEM((SUBLANES, f), F32)],
        compiler_params=_params("arbitrary"),
        name="mix_ffn",
    )(a, w_o, res, gain, w_in, conv_w, conv_b, w_out)


def _mla_proj_kernel(x_ref, gain_ref, w_in_ref, qn_ref, kvn_ref, wq_ref, wkv_ref,
                     c_ref, s_ref, gq_ref, gk_ref, q_ref, k_ref, v_ref):
    x = x_ref[...]
    h = (x * _rms_scale(x) * gain_ref[...]).astype(BF16)
    p = _dot(h, w_in_ref[...])
    c_q = p[:, :MLA_Q_RANK]
    c_kv = p[:, MLA_Q_RANK:MLA_Q_RANK + MLA_KV_RANK]
    k_r = p[:, MLA_Q_RANK + MLA_KV_RANK:MLA_Q_RANK + MLA_KV_RANK + LANES]
    k_s = p[:, MLA_Q_RANK + MLA_KV_RANK + LANES:]
    c_q = (c_q * _rms_scale(c_q) * qn_ref[...]).astype(BF16)
    c_kv = (c_kv * _rms_scale(c_kv) * kvn_ref[...]).astype(BF16)

    cos = c_ref[...]
    sin = s_ref[...]
    gq_nope, gq_rope, gq_swap = gq_ref[0:1, :], gq_ref[1:2, :], gq_ref[2:3, :]
    gk_nope, gk_rope, gk_swap = gk_ref[0:1, :], gk_ref[1:2, :], gk_ref[2:3, :]
    inv_dim = 1.0 / MLA_QK_DIM
    q_scale = MLA_QK_DIM ** -0.5 * LOG2_E

    k_rope_sq = k_r * k_r
    k_rope_rot = k_r * gk_rope * cos + k_s * gk_swap * sin

    def project(pair):
        q2 = _dot(c_q, wq_ref[:, pair * 2 * MLA_Q_COLS:(pair + 1) * 2 * MLA_Q_COLS])
        kv2 = _dot(c_kv, wkv_ref[:, pair * 2 * MLA_HEAD_PAD:(pair + 1) * 2 * MLA_HEAD_PAD])
        return q2, kv2

    def finish(pair, q2, kv2):
        for sub in range(2):
            hh = 2 * pair + sub
            q_n = q2[:, sub * MLA_Q_COLS:sub * MLA_Q_COLS + LANES]
            q_r = q2[:, sub * MLA_Q_COLS + LANES:sub * MLA_Q_COLS + 2 * LANES]
            q_s = q2[:, sub * MLA_Q_COLS + 2 * LANES:(sub + 1) * MLA_Q_COLS]
            ss = jnp.sum(q_n * q_n + q_r * q_r, axis=-1, keepdims=True)
            r = lax.rsqrt(ss * inv_dim + RMS_EPS) * q_scale
            q_ref[0, hh, :, 0:LANES] = (q_n * r * gq_nope).astype(BF16)
            q_ref[0, hh, :, LANES:2 * LANES] = ((q_r * gq_rope * cos + q_s * gq_swap * sin) * r).astype(BF16)

            k_n = kv2[:, sub * MLA_HEAD_PAD:sub * MLA_HEAD_PAD + LANES]
            ssk = jnp.sum(k_n * k_n + k_rope_sq, axis=-1, keepdims=True)
            rk = lax.rsqrt(ssk * inv_dim + RMS_EPS)
            k_ref[0, hh, :, 0:LANES] = (k_n * rk * gk_nope).astype(BF16)
            k_ref[0, hh, :, LANES:2 * LANES] = (k_rope_rot * rk).astype(BF16)
            v_ref[0, hh] = kv2[:, sub * MLA_HEAD_PAD + LANES:(sub + 1) * MLA_HEAD_PAD].astype(BF16)

    n_pairs = MLA_HEADS // 2
    pending = project(0)
    for pair in range(n_pairs):
        upcoming = project(pair + 1) if pair + 1 < n_pairs else None
        finish(pair, *pending)
        pending = upcoming


def _mla_proj(x2d, gain, w_in, qn, kvn, wq, wkv, cos, sin, gq, gk, batch, seq):
    t, d = x2d.shape
    tm = MLA_TM
    tps = seq // tm
    hd = MLA_HEADS

    return pl.pallas_call(
        _mla_proj_kernel,
        grid=(t // tm,),
        in_specs=[
            pl.BlockSpec((tm, d), lambda i: (i, 0)),
            _resident(gain), _resident(w_in), _resident(qn), _resident(kvn), _resident(wq), _resident(wkv),
            pl.BlockSpec((tm, LANES), lambda i: (i % tps, 0)),
            pl.BlockSpec((tm, LANES), lambda i: (i % tps, 0)),
            _resident(gq), _resident(gk),
        ],
        out_specs=[
            pl.BlockSpec((1, hd, tm, MLA_HEAD_PAD), lambda i: (i // tps, 0, i % tps, 0)),
            pl.BlockSpec((1, hd, tm, MLA_HEAD_PAD), lambda i: (i // tps, 0, i % tps, 0)),
            pl.BlockSpec((1, hd, tm, MLA_V_DIM), lambda i: (i // tps, 0, i % tps, 0)),
        ],
        out_shape=[
            jax.ShapeDtypeStruct((batch, hd, seq, MLA_HEAD_PAD), BF16),
            jax.ShapeDtypeStruct((batch, hd, seq, MLA_HEAD_PAD), BF16),
            jax.ShapeDtypeStruct((batch, hd, seq, MLA_V_DIM), BF16),
        ],
        compiler_params=_params("parallel"),
        name="mla_proj",
    )(x2d, gain, w_in, qn, kvn, wq, wkv, cos, sin, gq, gk)


def _attn_kernel(q_ref, k_ref, v_ref, o_ref, *, n_q_blocks):
    tq = ATT_TQ
    row = lax.broadcasted_iota(jnp.int32, (tq, tq), 0) // CHUNK
    col = lax.broadcasted_iota(jnp.int32, (tq, tq), 1) // CHUNK
    visible = col <= row

    def scores(qi):
        q0 = qi * tq
        q = q_ref[0, 0, q0:q0 + tq, :]
        s_diag = jnp.where(visible, _dot_nt(q, k_ref[0, 0, q0:q0 + tq, :]), MASK_VALUE)
        s_off = _dot_nt(q, k_ref[0, 0, 0:q0, :]) if qi > 0 else None
        return s_diag, s_off

    def finish(qi, s_diag, s_off):
        q0 = qi * tq
        m = jnp.max(s_diag, axis=-1, keepdims=True)
        if qi > 0:
            m = jnp.maximum(m, jnp.max(s_off, axis=-1, keepdims=True))
        p_diag = jnp.exp2(s_diag - m)
        l = jnp.sum(p_diag, axis=-1, keepdims=True)
        acc = _dot(p_diag.astype(BF16), v_ref[0, 0, q0:q0 + tq, :])
        if qi > 0:
            p_off = jnp.exp2(s_off - m)
            l = l + jnp.sum(p_off, axis=-1, keepdims=True)
            acc = acc + _dot(p_off.astype(BF16), v_ref[0, 0, 0:q0, :])
        o_ref[0, q0:q0 + tq, :] = (acc / l).astype(BF16)

    pending = scores(0)
    for qi in range(n_q_blocks):
        upcoming = scores(qi + 1) if qi + 1 < n_q_blocks else None
        finish(qi, *pending)
        pending = upcoming


def _attention(q, k, v):
    batch, hd, seq, _ = q.shape
    kern = functools.partial(_attn_kernel, n_q_blocks=seq // ATT_TQ)
    return pl.pallas_call(
        kern,
        grid=(batch, hd),
        in_specs=[
            pl.BlockSpec((1, 1, seq, MLA_HEAD_PAD), lambda b, h: (b, h, 0, 0)),
            pl.BlockSpec((1, 1, seq, MLA_HEAD_PAD), lambda b, h: (b, h, 0, 0)),
            pl.BlockSpec((1, 1, seq, MLA_V_DIM), lambda b, h: (b, h, 0, 0)),
        ],
        out_specs=pl.BlockSpec((1, seq, MLA_V_DIM), lambda b, h: (b, 0, h)),
        out_shape=jax.ShapeDtypeStruct((batch, seq, hd * MLA_V_DIM), BF16),
        compiler_params=_params("parallel", "parallel"),
        name="mla_attention",
    )(q, k, v)


def _rope_tables(seq, half):
    inv_freq = ROPE_THETA ** (-jnp.arange(half, dtype=F32) / half)
    ang = jnp.arange(seq).astype(F32)[:, None] * inv_freq[None, :]
    return jnp.cos(ang), jnp.sin(ang)


def _retention_decay_tables():
    sc = RET_SUPER
    log_gamma = jnp.log1p(-jnp.exp2(RET_GAMMA_BASE - jnp.arange(RET_HEADS, dtype=F32)))
    idx = jnp.arange(sc, dtype=F32)
    dist = jnp.abs(idx[:, None] - idx[None, :])
    chunk = jnp.arange(sc) // CHUNK
    visible = chunk[None, :] <= chunk[:, None]
    dmat = jnp.where(visible[None], jnp.exp(log_gamma[:, None, None] * dist[None]), 0.0)
    qd = jnp.exp(log_gamma[:, None] * (idx + 1.0))[:, :, None]
    kd = jnp.exp(log_gamma[:, None] * (sc - 1.0 - idx))[:, :, None]
    return dmat, qd, kd


def _swap_halves(w):
    half = w.shape[-1] // 2
    return jnp.concatenate([w[..., half:], w[..., :half]], axis=-1)


def _pad_lanes(w):
    return jnp.pad(w, [(0, 0)] * (w.ndim - 1) + [(0, LANES - w.shape[-1])])


def _mla_weights(w_in, w_qb, q_head_g, k_head_g):
    rq, rkv = MLA_Q_RANK, MLA_KV_RANK
    w_kr = w_in[:, rq + rkv:]
    w_in_l = jnp.concatenate([w_in[:, :rq + rkv], _pad_lanes(w_kr), _pad_lanes(_swap_halves(w_kr))], axis=1)
    w_qb3 = w_qb.reshape(rq, MLA_HEADS, MLA_QK_DIM)
    nope, rope = w_qb3[..., :MLA_NOPE_DIM], w_qb3[..., MLA_NOPE_DIM:]
    wq = jnp.concatenate([nope, _pad_lanes(rope), _pad_lanes(_swap_halves(rope))], axis=-1)
    wq = wq.reshape(rq, MLA_HEADS * MLA_Q_COLS)

    def gains(g):
        rope_g = g[MLA_NOPE_DIM:]
        return jnp.stack([g[:MLA_NOPE_DIM], _pad_lanes(rope_g), _pad_lanes(_swap_halves(rope_g))])

    return w_in_l, wq, gains(q_head_g), gains(k_head_g)


def kernel(x, ret_norm, ret_w_in, ret_gn, ret_w_out, mla_norm, mla_w_in, mla_q_norm, mla_w_qb, mla_kv_norm,
           mla_w_kvb, mla_q_head_norm, mla_k_head_norm, mla_w_out, ffn_norm, ffn_w_in, ffn_conv_w, ffn_conv_b,
           ffn_w_out):
    batch, seq, d = x.shape
    x2d = x.reshape(batch * seq, d)

    cos_r, sin_r = _rope_tables(seq, RET_QK_DIM // 2)
    dmat, qd, kd = _retention_decay_tables()
    qkvg = _ret_in_proj(x2d, ret_norm[0][None, :], ret_w_in[0].astype(BF16), cos_r, sin_r, seq)
    ret = _ret_core(qkvg, dmat, qd, kd, ret_gn[0][:, None, :], batch, seq)
    x2d = _mix_ffn(ret.reshape(batch * seq, -1), ret_w_out[0].astype(BF16), x2d, ffn_norm[0][None, :],
                   ffn_w_in[0].astype(BF16), ffn_conv_w[0], ffn_conv_b[0][None, :], ffn_w_out[0].astype(BF16), seq)

    cos_m, sin_m = _rope_tables(seq, MLA_ROPE_DIM // 2)
    pad = jnp.zeros((seq, LANES - MLA_ROPE_DIM), F32)
    cos_t = jnp.concatenate([cos_m, cos_m, pad], axis=1)
    sin_t = jnp.concatenate([-sin_m, sin_m, pad], axis=1)
    w_in_l, wq, gq, gk = _mla_weights(mla_w_in[0], mla_w_qb[0], mla_q_head_norm[0], mla_k_head_norm[0])
    q, k, v = _mla_proj(x2d, mla_norm[0][None, :], w_in_l.astype(BF16), mla_q_norm[0][None, :],
                        mla_kv_norm[0][None, :], wq.astype(BF16), mla_w_kvb[0].astype(BF16),
                        cos_t, sin_t, gq, gk, batch, seq)
    att = _attention(q, k, v)
    x2d = _mix_ffn(att.reshape(batch * seq, -1), mla_w_out[0].astype(BF16), x2d, ffn_norm[1][None, :],
                   ffn_w_in[1].astype(BF16), ffn_conv_w[1], ffn_conv_b[1][None, :], ffn_w_out[1].astype(BF16), seq)
    return x2d.reshape(batch, seq, d)
```

```python
import functools

import jax
import jax.numpy as jnp
from jax import lax
from jax.experimental import pallas as pl
from jax.experimental.pallas import tpu as pltpu

F32 = jnp.float32
BF16 = jnp.bfloat16

CHUNK = 64
RMS_EPS = 1e-6
ROPE_THETA = 10000.0
RET_HEADS = 4
RET_QK_DIM = 256
RET_V_DIM = 512
RET_GAMMA_BASE = -5.0
MLA_HEADS = 8
MLA_Q_RANK = 384
MLA_KV_RANK = 256
MLA_NOPE_DIM = 128
MLA_ROPE_DIM = 64
MLA_V_DIM = 128
MLA_QK_DIM = MLA_NOPE_DIM + MLA_ROPE_DIM
MLA_HEAD_PAD = 256
MLA_Q_COLS = 3 * 128
MASK_VALUE = -1e30
CONV_WIDTH = 3
LOG2_E = 1.4426950408889634

LANES = 128
SUBLANES = 8
VMEM_LIMIT_BYTES = 52 * 1024 * 1024

RET_SUPER = 256
PROJ_TM = 512
RET_HEADS_PER_STEP = 2
FFN_TM = 512
FFN_CHUNK = 256
MLA_TM = 512
ATT_TQ = 256
ATT_HEADS_PER_STEP = 2
ATT_AHEAD = 2


def _params(*sem):
    return pltpu.CompilerParams(dimension_semantics=sem, vmem_limit_bytes=VMEM_LIMIT_BYTES)


def _resident(arr):
    return pl.BlockSpec(arr.shape, lambda i: (0,) * arr.ndim, pipeline_mode=pl.Buffered(1))


def _rms_scale(x):
    return lax.rsqrt(jnp.mean(x * x, axis=-1, keepdims=True) + RMS_EPS)


def _silu(x):
    half = 0.5 * x
    return half + half * jnp.tanh(half)


def _dot(a, b):
    return jnp.dot(a, b, preferred_element_type=F32)


def _dot_nt(a, b):
    return lax.dot_general(a, b, (((1,), (1,)), ((), ())), preferred_element_type=F32)


def _dot_tn(a, b):
    return lax.dot_general(a, b, (((0,), (0,)), ((), ())), preferred_element_type=F32)


def _ret_in_kernel(x_ref, gain_ref, w_ref, cos_ref, sin_ref, o_ref, h_ref):
    half = RET_QK_DIM // 2
    x = x_ref[...]
    h_ref[...] = (x * _rms_scale(x) * gain_ref[...]).astype(BF16)
    cos = cos_ref[...]
    sin = sin_ref[...]
    k_scale = RET_QK_DIM ** -0.5
    cos_k = cos * k_scale
    sin_k = sin * k_scale
    for c in range(w_ref.shape[1] // RET_QK_DIM):
        lo = c * RET_QK_DIM
        acc = _dot(h_ref[...], w_ref[:, lo:lo + RET_QK_DIM])
        if c < 2 * RET_HEADS:
            cs, sn = (cos, sin) if c < RET_HEADS else (cos_k, sin_k)
            x1 = acc[:, :half]
            x2 = acc[:, half:]
            o_ref[:, lo:lo + half] = (x1 * cs - x2 * sn).astype(BF16)
            o_ref[:, lo + half:lo + RET_QK_DIM] = (x2 * cs + x1 * sn).astype(BF16)
        else:
            o_ref[:, lo:lo + RET_QK_DIM] = acc.astype(BF16)


def _ret_in_proj(x2d, gain, w, cos, sin, seq):
    t, d = x2d.shape
    n = w.shape[1]
    tm = PROJ_TM
    tps = seq // tm
    return pl.pallas_call(
        _ret_in_kernel,
        grid=(t // tm,),
        in_specs=[
            pl.BlockSpec((tm, d), lambda i: (i, 0)),
            _resident(gain), _resident(w),
            pl.BlockSpec((tm, RET_QK_DIM // 2), lambda i: (i % tps, 0)),
            pl.BlockSpec((tm, RET_QK_DIM // 2), lambda i: (i % tps, 0)),
        ],
        out_specs=pl.BlockSpec((tm, n), lambda i: (i, 0)),
        out_shape=jax.ShapeDtypeStruct((t, n), BF16),
        scratch_shapes=[pltpu.VMEM((tm, d), BF16)],
        compiler_params=_params("parallel"),
        name="ret_in_proj",
    )(x2d, gain, w, cos, sin)


def _ret_core_kernel(q_ref, k_ref, v_ref, g_ref, dmat_ref, qd_ref, kd_ref, gn_ref, o_ref, state_ref, *, n_steps):
    dk, dv = RET_QK_DIM, RET_V_DIM
    state_ref[...] = jnp.zeros_like(state_ref)

    for sc in range(n_steps):
        rows = slice(sc * RET_SUPER, (sc + 1) * RET_SUPER)
        for hh in range(RET_HEADS_PER_STEP):
            qd = qd_ref[hh]
            kd = kd_ref[hh]
            step_decay = qd[RET_SUPER - 1:RET_SUPER, :]
            q = q_ref[0, rows, hh * dk:(hh + 1) * dk]
            k = k_ref[0, rows, hh * dk:(hh + 1) * dk]
            v = v_ref[0, rows, hh * dv:(hh + 1) * dv]
            scores = _dot_nt(q, k) * dmat_ref[hh]
            inner = _dot(scores.astype(BF16), v)
            state = state_ref[hh]
            q_scaled = (q.astype(F32) * qd).astype(BF16)
            cross = _dot(q_scaled, state.astype(BF16))
            k_scaled = (k.astype(F32) * kd).astype(BF16)
            state_ref[hh] = state * step_decay + _dot_tn(k_scaled, v)
            out = inner + cross
            out = out * _rms_scale(out) * gn_ref[hh]
            g = g_ref[0, rows, hh * dv:(hh + 1) * dv].astype(F32)
            o_ref[0, rows, hh * dv:(hh + 1) * dv] = (out * _silu(g)).astype(BF16)


def _ret_core(qkvg, dmat, qd, kd, gn, batch, seq):
    hd, dk, dv = RET_HEADS, RET_QK_DIM, RET_V_DIM
    hps = RET_HEADS_PER_STEP
    groups = hd // hps
    qkvg3 = qkvg.reshape(batch, seq, qkvg.shape[-1])
    k_blk0 = groups
    v_blk0 = 2 * hd * dk // (hps * dv)
    g_blk0 = v_blk0 + groups
    kern = functools.partial(_ret_core_kernel, n_steps=seq // RET_SUPER)
    return pl.pallas_call(
        kern,
        grid=(batch, groups),
        in_specs=[
            pl.BlockSpec((1, seq, hps * dk), lambda b, h: (b, 0, h)),
            pl.BlockSpec((1, seq, hps * dk), lambda b, h: (b, 0, k_blk0 + h)),
            pl.BlockSpec((1, seq, hps * dv), lambda b, h: (b, 0, v_blk0 + h)),
            pl.BlockSpec((1, seq, hps * dv), lambda b, h: (b, 0, g_blk0 + h)),
            pl.BlockSpec((hps, RET_SUPER, RET_SUPER), lambda b, h: (h, 0, 0)),
            pl.BlockSpec((hps, RET_SUPER, 1), lambda b, h: (h, 0, 0)),
            pl.BlockSpec((hps, RET_SUPER, 1), lambda b, h: (h, 0, 0)),
            pl.BlockSpec((hps, 1, dv), lambda b, h: (h, 0, 0)),
        ],
        out_specs=pl.BlockSpec((1, seq, hps * dv), lambda b, h: (b, 0, h)),
        out_shape=jax.ShapeDtypeStruct((batch, seq, hd * dv), BF16),
        scratch_shapes=[pltpu.VMEM((hps, dk, dv), F32)],
        compiler_params=_params("parallel", "parallel"),
        name="ret_core",
    )(qkvg3, qkvg3, qkvg3, qkvg3, dmat, qd, kd, gn)


def _mix_ffn_kernel(a_ref, wo_ref, res_ref, gain_ref, w_in_ref, cw_ref, cb_ref, w_out_ref, o_ref,
                    h_ref, act_ref, carry_ref, *, tiles_per_seq):
    i = pl.program_id(0)
    tm = res_ref.shape[0]
    ffn_dim = w_out_ref.shape[0]

    x1 = res_ref[...] + _dot(a_ref[...], wo_ref[...])
    o_ref[...] = x1
    h_ref[...] = (x1 * _rms_scale(x1) * gain_ref[...]).astype(BF16)

    seq_start = lax.rem(i, tiles_per_seq) == 0
    row = lax.broadcasted_iota(jnp.int32, (tm, 1), 0)
    for c in range(ffn_dim // FFN_CHUNK):
        lo = c * FFN_CHUNK
        a = _dot(h_ref[...], w_in_ref[:, lo:lo + FFN_CHUNK])
        g = _dot(h_ref[...], w_in_ref[:, ffn_dim + lo:ffn_dim + lo + FFN_CHUNK])
        prev = jnp.where(seq_start, 0.0, carry_ref[:, lo:lo + FFN_CHUNK])
        prev1 = prev[SUBLANES - 1:SUBLANES, :]
        prev2 = prev[SUBLANES - 2:SUBLANES - 1, :]
        g1 = jnp.where(row == 0, prev1, pltpu.roll(g, 1, 0))
        g2 = jnp.where(row == 0, prev2, jnp.where(row == 1, prev1, pltpu.roll(g, 2, 0)))
        carry_ref[:, lo:lo + FFN_CHUNK] = g[tm - SUBLANES:tm, :]
        cw = cw_ref[:, lo:lo + FFN_CHUNK]
        gc = g2 * cw[0:1, :] + g1 * cw[1:2, :] + g * cw[2:3, :] + cb_ref[:, lo:lo + FFN_CHUNK]
        act_ref[:, lo:lo + FFN_CHUNK] = (_silu(gc) * a).astype(BF16)

    o_ref[...] += _dot(act_ref[...], w_out_ref[...])


def _mix_ffn(a, w_o, res, gain, w_in, conv_w, conv_b, w_out, seq):
    t, d = res.shape
    ka = a.shape[1]
    f = w_out.shape[0]
    tm = FFN_TM
    kern = functools.partial(_mix_ffn_kernel, tiles_per_seq=seq // tm)
    return pl.pallas_call(
        kern,
        grid=(t // tm,),
        in_specs=[
            pl.BlockSpec((tm, ka), lambda i: (i, 0)),
            _resident(w_o),
            pl.BlockSpec((tm, d), lambda i: (i, 0)),
            _resident(gain), _resident(w_in), _resident(conv_w), _resident(conv_b), _resident(w_out),
        ],
        out_specs=pl.BlockSpec((tm, d), lambda i: (i, 0)),
        out_shape=jax.ShapeDtypeStruct((t, d), F32),
        scratch_shapes=[pltpu.VMEM((tm, d), BF16), pltpu.VMEM((tm, f), BF16), pltpu.VMEM((SUBLANES, f), F32)],
        compiler_params=_params("arbitrary"),
        name="mix_ffn",
    )(a, w_o, res, gain, w_in, conv_w, conv_b, w_out)


def _mla_proj_kernel(x_ref, gain_ref, w_in_ref, qn_ref, kvn_ref, wq_ref, wkv_ref,
                     c_ref, s_ref, gq_ref, gk_ref, q_ref, k_ref, v_ref):
    x = x_ref[...]
    h = (x * _rms_scale(x) * gain_ref[...]).astype(BF16)
    p = _dot(h, w_in_ref[...])
    c_q = p[:, :MLA_Q_RANK]
    c_kv = p[:, MLA_Q_RANK:MLA_Q_RANK + MLA_KV_RANK]
    k_r = p[:, MLA_Q_RANK + MLA_KV_RANK:MLA_Q_RANK + MLA_KV_RANK + LANES]
    k_s = p[:, MLA_Q_RANK + MLA_KV_RANK + LANES:]
    c_q = (c_q * _rms_scale(c_q) * qn_ref[...]).astype(BF16)
    c_kv = (c_kv * _rms_scale(c_kv) * kvn_ref[...]).astype(BF16)

    cos = c_ref[...]
    sin = s_ref[...]
    gq_nope, gq_rope, gq_swap = gq_ref[0:1, :], gq_ref[1:2, :], gq_ref[2:3, :]
    gk_nope, gk_rope, gk_swap = gk_ref[0:1, :], gk_ref[1:2, :], gk_ref[2:3, :]
    inv_dim = 1.0 / MLA_QK_DIM
    q_scale = MLA_QK_DIM ** -0.5 * LOG2_E

    k_rope_sq = k_r * k_r
    k_rope_rot = k_r * gk_rope * cos + k_s * gk_swap * sin

    def project(pair):
        q2 = _dot(c_q, wq_ref[:, pair * 2 * MLA_Q_COLS:(pair + 1) * 2 * MLA_Q_COLS])
        kv2 = _dot(c_kv, wkv_ref[:, pair * 2 * MLA_HEAD_PAD:(pair + 1) * 2 * MLA_HEAD_PAD])
        return q2, kv2

    def finish(pair, q2, kv2):
        for sub in range(2):
            hh = 2 * pair + sub
            q_n = q2[:, sub * MLA_Q_COLS:sub * MLA_Q_COLS + LANES]
            q_r = q2[:, sub * MLA_Q_COLS + LANES:sub * MLA_Q_COLS + 2 * LANES]
            q_s = q2[:, sub * MLA_Q_COLS + 2 * LANES:(sub + 1) * MLA_Q_COLS]
            ss = jnp.sum(q_n * q_n + q_r * q_r, axis=-1, keepdims=True)
            r = lax.rsqrt(ss * inv_dim + RMS_EPS) * q_scale
            q_ref[0, hh, :, 0:LANES] = (q_n * r * gq_nope).astype(BF16)
            q_ref[0, hh, :, LANES:2 * LANES] = ((q_r * gq_rope * cos + q_s * gq_swap * sin) * r).astype(BF16)

            k_n = kv2[:, sub * MLA_HEAD_PAD:sub * MLA_HEAD_PAD + LANES]
            ssk = jnp.sum(k_n * k_n + k_rope_sq, axis=-1, keepdims=True)
            rk = lax.rsqrt(ssk * inv_dim + RMS_EPS)
            k_ref[0, hh, :, 0:LANES] = (k_n * rk * gk_nope).astype(BF16)
            k_ref[0, hh, :, LANES:2 * LANES] = (k_rope_rot * rk).astype(BF16)
            v_ref[0, hh] = kv2[:, sub * MLA_HEAD_PAD + LANES:(sub + 1) * MLA_HEAD_PAD].astype(BF16)

    n_pairs = MLA_HEADS // 2
    pending = project(0)
    for pair in range(n_pairs):
        upcoming = project(pair + 1) if pair + 1 < n_pairs else None
        finish(pair, *pending)
        pending = upcoming


def _mla_proj(x2d, gain, w_in, qn, kvn, wq, wkv, cos, sin, gq, gk, batch, seq):
    t, d = x2d.shape
    tm = MLA_TM
    tps = seq // tm
    hd = MLA_HEADS

    return pl.pallas_call(
        _mla_proj_kernel,
        grid=(t // tm,),
        in_specs=[
            pl.BlockSpec((tm, d), lambda i: (i, 0)),
            _resident(gain), _resident(w_in), _resident(qn), _resident(kvn), _resident(wq), _resident(wkv),
            pl.BlockSpec((tm, LANES), lambda i: (i % tps, 0)),
            pl.BlockSpec((tm, LANES), lambda i: (i % tps, 0)),
            _resident(gq), _resident(gk),
        ],
        out_specs=[
            pl.BlockSpec((1, hd, tm, MLA_HEAD_PAD), lambda i: (i // tps, 0, i % tps, 0)),
            pl.BlockSpec((1, hd, tm, MLA_HEAD_PAD), lambda i: (i // tps, 0, i % tps, 0)),
            pl.BlockSpec((1, hd, tm, MLA_V_DIM), lambda i: (i // tps, 0, i % tps, 0)),
        ],
        out_shape=[
            jax.ShapeDtypeStruct((batch, hd, seq, MLA_HEAD_PAD), BF16),
            jax.ShapeDtypeStruct((batch, hd, seq, MLA_HEAD_PAD), BF16),
            jax.ShapeDtypeStruct((batch, hd, seq, MLA_V_DIM), BF16),
        ],
        compiler_params=_params("parallel"),
        name="mla_proj",
    )(x2d, gain, w_in, qn, kvn, wq, wkv, cos, sin, gq, gk)


def _attn_kernel(q_ref, k_ref, v_ref, o_ref, *, n_q_blocks):
    tq = ATT_TQ
    row = lax.broadcasted_iota(jnp.int32, (tq, tq), 0) // CHUNK
    col = lax.broadcasted_iota(jnp.int32, (tq, tq), 1) // CHUNK
    visible = col <= row

    def scores(hh, qi):
        q0 = qi * tq
        q = q_ref[0, hh, q0:q0 + tq, :]
        s_diag = jnp.where(visible, _dot_nt(q, k_ref[0, hh, q0:q0 + tq, :]), MASK_VALUE)
        s_off = _dot_nt(q, k_ref[0, hh, 0:q0, :]) if qi > 0 else None
        return s_diag, s_off

    def finish(hh, qi, s_diag, s_off):
        q0 = qi * tq
        m = jnp.max(s_diag, axis=-1, keepdims=True)
        if qi > 0:
            m = jnp.maximum(m, jnp.max(s_off, axis=-1, keepdims=True))
        p_diag = jnp.exp2(s_diag - m)
        l = jnp.sum(p_diag, axis=-1, keepdims=True)
        acc = _dot(p_diag.astype(BF16), v_ref[0, hh, q0:q0 + tq, :])
        if qi > 0:
            p_off = jnp.exp2(s_off - m)
            l = l + jnp.sum(p_off, axis=-1, keepdims=True)
            acc = acc + _dot(p_off.astype(BF16), v_ref[0, hh, 0:q0, :])
        o_ref[0, q0:q0 + tq, hh * MLA_V_DIM:(hh + 1) * MLA_V_DIM] = (acc / l).astype(BF16)

    work = [(hh, qi) for hh in range(ATT_HEADS_PER_STEP) for qi in range(n_q_blocks - 1, -1, -1)]
    queue = [scores(*item) for item in work[:ATT_AHEAD]]
    for pos, item in enumerate(work):
        if pos + ATT_AHEAD < len(work):
            queue.append(scores(*work[pos + ATT_AHEAD]))
        finish(*item, *queue.pop(0))


def _attention(q, k, v):
    batch, hd, seq, _ = q.shape
    hps = ATT_HEADS_PER_STEP
    kern = functools.partial(_attn_kernel, n_q_blocks=seq // ATT_TQ)
    return pl.pallas_call(
        kern,
        grid=(batch, hd // hps),
        in_specs=[
            pl.BlockSpec((1, hps, seq, MLA_HEAD_PAD), lambda b, h: (b, h, 0, 0)),
            pl.BlockSpec((1, hps, seq, MLA_HEAD_PAD), lambda b, h: (b, h, 0, 0)),
            pl.BlockSpec((1, hps, seq, MLA_V_DIM), lambda b, h: (b, h, 0, 0)),
        ],
        out_specs=pl.BlockSpec((1, seq, hps * MLA_V_DIM), lambda b, h: (b, 0, h)),
        out_shape=jax.ShapeDtypeStruct((batch, seq, hd * MLA_V_DIM), BF16),
        compiler_params=_params("parallel", "parallel"),
        name="mla_attention",
    )(q, k, v)


def _rope_tables(seq, half):
    inv_freq = ROPE_THETA ** (-jnp.arange(half, dtype=F32) / half)
    ang = jnp.arange(seq).astype(F32)[:, None] * inv_freq[None, :]
    return jnp.cos(ang), jnp.sin(ang)


def _retention_decay_tables():
    sc = RET_SUPER
    log_gamma = jnp.log1p(-jnp.exp2(RET_GAMMA_BASE - jnp.arange(RET_HEADS, dtype=F32)))
    idx = jnp.arange(sc, dtype=F32)
    dist = jnp.abs(idx[:, None] - idx[None, :])
    chunk = jnp.arange(sc) // CHUNK
    visible = chunk[None, :] <= chunk[:, None]
    dmat = jnp.where(visible[None], jnp.exp(log_gamma[:, None, None] * dist[None]), 0.0)
    qd = jnp.exp(log_gamma[:, None] * (idx + 1.0))[:, :, None]
    kd = jnp.exp(log_gamma[:, None] * (sc - 1.0 - idx))[:, :, None]
    return dmat, qd, kd


def _swap_halves(w):
    half = w.shape[-1] // 2
    return jnp.concatenate([w[..., half:], w[..., :half]], axis=-1)


def _pad_lanes(w):
    return jnp.pad(w, [(0, 0)] * (w.ndim - 1) + [(0, LANES - w.shape[-1])])


def _mla_weights(w_in, w_qb, q_head_g, k_head_g):
    rq, rkv = MLA_Q_RANK, MLA_KV_RANK
    w_kr = w_in[:, rq + rkv:]
    w_in_l = jnp.concatenate([w_in[:, :rq + rkv], _pad_lanes(w_kr), _pad_lanes(_swap_halves(w_kr))], axis=1)
    w_qb3 = w_qb.reshape(rq, MLA_HEADS, MLA_QK_DIM)
    nope, rope = w_qb3[..., :MLA_NOPE_DIM], w_qb3[..., MLA_NOPE_DIM:]
    wq = jnp.concatenate([nope, _pad_lanes(rope), _pad_lanes(_swap_halves(rope))], axis=-1)
    wq = wq.reshape(rq, MLA_HEADS * MLA_Q_COLS)

    def gains(g):
        rope_g = g[MLA_NOPE_DIM:]
        return jnp.stack([g[:MLA_NOPE_DIM], _pad_lanes(rope_g), _pad_lanes(_swap_halves(rope_g))])

    return w_in_l, wq, gains(q_head_g), gains(k_head_g)


def kernel(x, ret_norm, ret_w_in, ret_gn, ret_w_out, mla_norm, mla_w_in, mla_q_norm, mla_w_qb, mla_kv_norm,
           mla_w_kvb, mla_q_head_norm, mla_k_head_norm, mla_w_out, ffn_norm, ffn_w_in, ffn_conv_w, ffn_conv_b,
           ffn_w_out):
    batch, seq, d = x.shape
    x2d = x.reshape(batch * seq, d)

    cos_r, sin_r = _rope_tables(seq, RET_QK_DIM // 2)
    dmat, qd, kd = _retention_decay_tables()
    qkvg = _ret_in_proj(x2d, ret_norm[0][None, :], ret_w_in[0].astype(BF16), cos_r, sin_r, seq)
    ret = _ret_core(qkvg, dmat, qd, kd, ret_gn[0][:, None, :], batch, seq)
    x2d = _mix_ffn(ret.reshape(batch * seq, -1), ret_w_out[0].astype(BF16), x2d, ffn_norm[0][None, :],
                   ffn_w_in[0].astype(BF16), ffn_conv_w[0], ffn_conv_b[0][None, :], ffn_w_out[0].astype(BF16), seq)

    cos_m, sin_m = _rope_tables(seq, MLA_ROPE_DIM // 2)
    pad = jnp.zeros((seq, LANES - MLA_ROPE_DIM), F32)
    cos_t = jnp.concatenate([cos_m, cos_m, pad], axis=1)
    sin_t = jnp.concatenate([-sin_m, sin_m, pad], axis=1)
    w_in_l, wq, gq, gk = _mla_weights(mla_w_in[0], mla_w_qb[0], mla_q_head_norm[0], mla_k_head_norm[0])
    q, k, v = _mla_proj(x2d, mla_norm[0][None, :], w_in_l.astype(BF16), mla_q_norm[0][None, :],
                        mla_kv_norm[0][None, :], wq.astype(BF16), mla_w_kvb[0].astype(BF16),
                        cos_t, sin_t, gq, gk, batch, seq)
    att = _attention(q, k, v)
    x2d = _mix_ffn(att.reshape(batch * seq, -1), mla_w_out[0].astype(BF16), x2d, ffn_norm[1][None, :],
                   ffn_w_in[1].astype(BF16), ffn_conv_w[1], ffn_conv_b[1][None, :], ffn_w_out[1].astype(BF16), seq)
    return x2d.reshape(batch, seq, d)
```

```python
import functools

import jax
import jax.numpy as jnp
from jax import lax
from jax.experimental import pallas as pl
from jax.experimental.pallas import tpu as pltpu

F32 = jnp.float32
BF16 = jnp.bfloat16

CHUNK = 64
RMS_EPS = 1e-6
ROPE_THETA = 10000.0
RET_HEADS = 4
RET_QK_DIM = 256
RET_V_DIM = 512
RET_GAMMA_BASE = -5.0
MLA_HEADS = 8
MLA_Q_RANK = 384
MLA_KV_RANK = 256
MLA_NOPE_DIM = 128
MLA_ROPE_DIM = 64
MLA_V_DIM = 128
MLA_QK_DIM = MLA_NOPE_DIM + MLA_ROPE_DIM
MLA_HEAD_PAD = 256
MLA_Q_COLS = 3 * 128
MASK_VALUE = -1e30
CONV_WIDTH = 3
LOG2_E = 1.4426950408889634

LANES = 128
SUBLANES = 8
VMEM_LIMIT_BYTES = 52 * 1024 * 1024
BF16_ROWS = 16

RET_SUPER = 256
PROJ_TM = 512
RET_HEADS_PER_STEP = 2
FFN_TM = 512
FFN_CHUNK = 256
MLA_TM = 512
W_STAGE_BYTES = 768 * 1024
ATT_TQ = 256
ATT_HEADS_PER_STEP = 2
ATT_AHEAD = 2


def _params(*sem):
    return pltpu.CompilerParams(dimension_semantics=sem, vmem_limit_bytes=VMEM_LIMIT_BYTES)


def _resident(arr):
    return pl.BlockSpec(arr.shape, lambda i: (0,) * arr.ndim, pipeline_mode=pl.Buffered(1))


def _stage_rows(rows, cols):
    limit = W_STAGE_BYTES // (cols * 4)
    fits = [r for r in range(BF16_ROWS, limit + 1, BF16_ROWS) if rows % r == 0]
    return max(fits)


def _weight_scratch(rows, cols):
    return [pltpu.VMEM((rows, cols), BF16), pltpu.VMEM((2, _stage_rows(rows, cols), cols), F32),
            pltpu.SemaphoreType.DMA((2,))]


def _load_weight(src_ref, dst_ref, stage_ref, sem_ref):
    rows = stage_ref.shape[1]
    n_chunks = dst_ref.shape[0] // rows

    def copy(k, slot):
        return pltpu.make_async_copy(src_ref.at[pl.ds(k * rows, rows), :], stage_ref.at[slot], sem_ref.at[slot])

    copy(0, 0).start()

    def body(k, carry):
        slot = lax.rem(k, 2)

        @pl.when(k + 1 < n_chunks)
        def _():
            copy(k + 1, 1 - slot).start()

        copy(k, slot).wait()
        r0 = pl.multiple_of(k * rows, rows)
        dst_ref[pl.ds(r0, rows), :] = stage_ref[slot].astype(BF16)
        return carry

    lax.fori_loop(0, n_chunks, body, 0)


def _rms_scale(x):
    return lax.rsqrt(jnp.mean(x * x, axis=-1, keepdims=True) + RMS_EPS)


def _silu(x):
    half = 0.5 * x
    return half + half * jnp.tanh(half)


def _dot(a, b):
    return jnp.dot(a, b, preferred_element_type=F32)


def _dot_nt(a, b):
    return lax.dot_general(a, b, (((1,), (1,)), ((), ())), preferred_element_type=F32)


def _dot_tn(a, b):
    return lax.dot_general(a, b, (((0,), (0,)), ((), ())), preferred_element_type=F32)


def _ret_in_kernel(x_ref, gain_ref, w_hbm, cos_ref, sin_ref, o_ref, h_ref, w_ref, stage_ref, sem_ref):
    half = RET_QK_DIM // 2

    @pl.when(pl.program_id(0) == 0)
    def _():
        _load_weight(w_hbm.at[0], w_ref, stage_ref, sem_ref)

    x = x_ref[...]
    h_ref[...] = (x * _rms_scale(x) * gain_ref[...]).astype(BF16)
    cos = cos_ref[...]
    sin = sin_ref[...]
    k_scale = RET_QK_DIM ** -0.5
    cos_k = cos * k_scale
    sin_k = sin * k_scale
    for c in range(w_ref.shape[1] // RET_QK_DIM):
        lo = c * RET_QK_DIM
        acc = _dot(h_ref[...], w_ref[:, lo:lo + RET_QK_DIM])
        if c < 2 * RET_HEADS:
            cs, sn = (cos, sin) if c < RET_HEADS else (cos_k, sin_k)
            x1 = acc[:, :half]
            x2 = acc[:, half:]
            o_ref[:, lo:lo + half] = (x1 * cs - x2 * sn).astype(BF16)
            o_ref[:, lo + half:lo + RET_QK_DIM] = (x2 * cs + x1 * sn).astype(BF16)
        else:
            o_ref[:, lo:lo + RET_QK_DIM] = acc.astype(BF16)


def _ret_in_proj(x2d, gain, w, cos, sin, seq):
    t, d = x2d.shape
    n = w.shape[-1]
    tm = PROJ_TM
    tps = seq // tm
    return pl.pallas_call(
        _ret_in_kernel,
        grid=(t // tm,),
        in_specs=[
            pl.BlockSpec((tm, d), lambda i: (i, 0)),
            _resident(gain), pl.BlockSpec(memory_space=pl.ANY),
            pl.BlockSpec((tm, RET_QK_DIM // 2), lambda i: (i % tps, 0)),
            pl.BlockSpec((tm, RET_QK_DIM // 2), lambda i: (i % tps, 0)),
        ],
        out_specs=pl.BlockSpec((tm, n), lambda i: (i, 0)),
        out_shape=jax.ShapeDtypeStruct((t, n), BF16),
        scratch_shapes=[pltpu.VMEM((tm, d), BF16)] + _weight_scratch(d, n),
        compiler_params=_params("arbitrary"),
        name="ret_in_proj",
    )(x2d, gain, w, cos, sin)


def _ret_core_kernel(q_ref, k_ref, v_ref, g_ref, dmat_ref, qd_ref, kd_ref, gn_ref, o_ref, state_ref, *, n_steps):
    dk, dv = RET_QK_DIM, RET_V_DIM
    state_ref[...] = jnp.zeros_like(state_ref)

    for sc in range(n_steps):
        rows = slice(sc * RET_SUPER, (sc + 1) * RET_SUPER)
        for hh in range(RET_HEADS_PER_STEP):
            qd = qd_ref[hh]
            kd = kd_ref[hh]
            step_decay = qd[RET_SUPER - 1:RET_SUPER, :]
            q = q_ref[0, rows, hh * dk:(hh + 1) * dk]
            k = k_ref[0, rows, hh * dk:(hh + 1) * dk]
            v = v_ref[0, rows, hh * dv:(hh + 1) * dv]
            scores = _dot_nt(q, k) * dmat_ref[hh]
            inner = _dot(scores.astype(BF16), v)
            state = state_ref[hh]
            q_scaled = (q.astype(F32) * qd).astype(BF16)
            cross = _dot(q_scaled, state.astype(BF16))
            k_scaled = (k.astype(F32) * kd).astype(BF16)
            state_ref[hh] = state * step_decay + _dot_tn(k_scaled, v)
            out = inner + cross
            out = out * _rms_scale(out) * gn_ref[hh]
            g = g_ref[0, rows, hh * dv:(hh + 1) * dv].astype(F32)
            o_ref[0, rows, hh * dv:(hh + 1) * dv] = (out * _silu(g)).astype(BF16)


def _ret_core(qkvg, dmat, qd, kd, gn, batch, seq):
    hd, dk, dv = RET_HEADS, RET_QK_DIM, RET_V_DIM
    hps = RET_HEADS_PER_STEP
    groups = hd // hps
    qkvg3 = qkvg.reshape(batch, seq, qkvg.shape[-1])
    k_blk0 = groups
    v_blk0 = 2 * hd * dk // (hps * dv)
    g_blk0 = v_blk0 + groups
    kern = functools.partial(_ret_core_kernel, n_steps=seq // RET_SUPER)
    return pl.pallas_call(
        kern,
        grid=(batch, groups),
        in_specs=[
            pl.BlockSpec((1, seq, hps * dk), lambda b, h: (b, 0, h)),
            pl.BlockSpec((1, seq, hps * dk), lambda b, h: (b, 0, k_blk0 + h)),
            pl.BlockSpec((1, seq, hps * dv), lambda b, h: (b, 0, v_blk0 + h)),
            pl.BlockSpec((1, seq, hps * dv), lambda b, h: (b, 0, g_blk0 + h)),
            pl.BlockSpec((hps, RET_SUPER, RET_SUPER), lambda b, h: (h, 0, 0)),
            pl.BlockSpec((hps, RET_SUPER, 1), lambda b, h: (h, 0, 0)),
            pl.BlockSpec((hps, RET_SUPER, 1), lambda b, h: (h, 0, 0)),
            pl.BlockSpec((hps, 1, dv), lambda b, h: (h, 0, 0)),
        ],
        out_specs=pl.BlockSpec((1, seq, hps * dv), lambda b, h: (b, 0, h)),
        out_shape=jax.ShapeDtypeStruct((batch, seq, hd * dv), BF16),
        scratch_shapes=[pltpu.VMEM((hps, dk, dv), F32)],
        compiler_params=_params("parallel", "parallel"),
        name="ret_core",
    )(qkvg3, qkvg3, qkvg3, qkvg3, dmat, qd, kd, gn)


def _mix_ffn_kernel(a_ref, wo_hbm, res_ref, gain_ref, w_in_hbm, cw_ref, cb_ref, w_out_hbm, o_ref,
                    h_ref, act_ref, carry_ref, wo_ref, wo_stage, wo_sem, w_in_ref, w_in_stage, w_in_sem,
                    w_out_ref, w_out_stage, w_out_sem, *, tiles_per_seq, layer):
    i = pl.program_id(0)
    tm = res_ref.shape[0]
    ffn_dim = w_out_ref.shape[0]

    @pl.when(i == 0)
    def _():
        _load_weight(wo_hbm.at[0], wo_ref, wo_stage, wo_sem)
        _load_weight(w_in_hbm.at[layer], w_in_ref, w_in_stage, w_in_sem)
        _load_weight(w_out_hbm.at[layer], w_out_ref, w_out_stage, w_out_sem)

    x1 = res_ref[...] + _dot(a_ref[...], wo_ref[...])
    o_ref[...] = x1
    h_ref[...] = (x1 * _rms_scale(x1) * gain_ref[...]).astype(BF16)

    seq_start = lax.rem(i, tiles_per_seq) == 0
    row = lax.broadcasted_iota(jnp.int32, (tm, 1), 0)
    for c in range(ffn_dim // FFN_CHUNK):
        lo = c * FFN_CHUNK
        a = _dot(h_ref[...], w_in_ref[:, lo:lo + FFN_CHUNK])
        g = _dot(h_ref[...], w_in_ref[:, ffn_dim + lo:ffn_dim + lo + FFN_CHUNK])
        prev = jnp.where(seq_start, 0.0, carry_ref[:, lo:lo + FFN_CHUNK])
        prev1 = prev[SUBLANES - 1:SUBLANES, :]
        prev2 = prev[SUBLANES - 2:SUBLANES - 1, :]
        g1 = jnp.where(row == 0, prev1, pltpu.roll(g, 1, 0))
        g2 = jnp.where(row == 0, prev2, jnp.where(row == 1, prev1, pltpu.roll(g, 2, 0)))
        carry_ref[:, lo:lo + FFN_CHUNK] = g[tm - SUBLANES:tm, :]
        cw = cw_ref[:, lo:lo + FFN_CHUNK]
        gc = g2 * cw[0:1, :] + g1 * cw[1:2, :] + g * cw[2:3, :] + cb_ref[:, lo:lo + FFN_CHUNK]
        act_ref[:, lo:lo + FFN_CHUNK] = (_silu(gc) * a).astype(BF16)

    o_ref[...] += _dot(act_ref[...], w_out_ref[...])


def _mix_ffn(a, w_o, res, gain, w_in, conv_w, conv_b, w_out, seq, layer):
    t, d = res.shape
    ka = a.shape[1]
    f = w_out.shape[1]
    tm = FFN_TM
    hbm = pl.BlockSpec(memory_space=pl.ANY)
    kern = functools.partial(_mix_ffn_kernel, tiles_per_seq=seq // tm, layer=layer)
    return pl.pallas_call(
        kern,
        grid=(t // tm,),
        in_specs=[
            pl.BlockSpec((tm, ka), lambda i: (i, 0)),
            hbm,
            pl.BlockSpec((tm, d), lambda i: (i, 0)),
            _resident(gain), hbm, _resident(conv_w), _resident(conv_b), hbm,
        ],
        out_specs=pl.BlockSpec((tm, d), lambda i: (i, 0)),
        out_shape=jax.ShapeDtypeStruct((t, d), F32),
        scratch_shapes=([pltpu.VMEM((tm, d), BF16), pltpu.VMEM((tm, f), BF16), pltpu.VMEM((SUBLANES, f), F32)]
                        + _weight_scratch(ka, d) + _weight_scratch(d, 2 * f) + _weight_scratch(f, d)),
        compiler_params=_params("arbitrary"),
        name="mix_ffn",
    )(a, w_o, res, gain, w_in, conv_w, conv_b, w_out)


def _mla_proj_kernel(x_ref, gain_ref, w_in_ref, qn_ref, kvn_ref, wq_ref, wkv_ref,
                     c_ref, s_ref, gq_ref, gk_ref, q_ref, k_ref, v_ref):
    x = x_ref[...]
    h = (x * _rms_scale(x) * gain_ref[...]).astype(BF16)
    p = _dot(h, w_in_ref[...])
    c_q = p[:, :MLA_Q_RANK]
    c_kv = p[:, MLA_Q_RANK:MLA_Q_RANK + MLA_KV_RANK]
    k_r = p[:, MLA_Q_RANK + MLA_KV_RANK:MLA_Q_RANK + MLA_KV_RANK + LANES]
    k_s = p[:, MLA_Q_RANK + MLA_KV_RANK + LANES:]
    c_q = (c_q * _rms_scale(c_q) * qn_ref[...]).astype(BF16)
    c_kv = (c_kv * _rms_scale(c_kv) * kvn_ref[...]).astype(BF16)

    cos = c_ref[...]
    sin = s_ref[...]
    gq_nope, gq_rope, gq_swap = gq_ref[0:1, :], gq_ref[1:2, :], gq_ref[2:3, :]
    gk_nope, gk_rope, gk_swap = gk_ref[0:1, :], gk_ref[1:2, :], gk_ref[2:3, :]
    inv_dim = 1.0 / MLA_QK_DIM
    q_scale = MLA_QK_DIM ** -0.5 * LOG2_E

    k_rope_sq = k_r * k_r
    k_rope_rot = k_r * gk_rope * cos + k_s * gk_swap * sin

    def project(pair):
        q2 = _dot(c_q, wq_ref[:, pair * 2 * MLA_Q_COLS:(pair + 1) * 2 * MLA_Q_COLS])
        kv2 = _dot(c_kv, wkv_ref[:, pair * 2 * MLA_HEAD_PAD:(pair + 1) * 2 * MLA_HEAD_PAD])
        return q2, kv2

    def finish(pair, q2, kv2):
        for sub in range(2):
            hh = 2 * pair + sub
            q_n = q2[:, sub * MLA_Q_COLS:sub * MLA_Q_COLS + LANES]
            q_r = q2[:, sub * MLA_Q_COLS + LANES:sub * MLA_Q_COLS + 2 * LANES]
            q_s = q2[:, sub * MLA_Q_COLS + 2 * LANES:(sub + 1) * MLA_Q_COLS]
            ss = jnp.sum(q_n * q_n + q_r * q_r, axis=-1, keepdims=True)
            r = lax.rsqrt(ss * inv_dim + RMS_EPS) * q_scale
            q_ref[0, hh, :, 0:LANES] = (q_n * r * gq_nope).astype(BF16)
            q_ref[0, hh, :, LANES:2 * LANES] = ((q_r * gq_rope * cos + q_s * gq_swap * sin) * r).astype(BF16)

            k_n = kv2[:, sub * MLA_HEAD_PAD:sub * MLA_HEAD_PAD + LANES]
            ssk = jnp.sum(k_n * k_n + k_rope_sq, axis=-1, keepdims=True)
            rk = lax.rsqrt(ssk * inv_dim + RMS_EPS)
            k_ref[0, hh, :, 0:LANES] = (k_n * rk * gk_nope).astype(BF16)
            k_ref[0, hh, :, LANES:2 * LANES] = (k_rope_rot * rk).astype(BF16)
            v_ref[0, hh] = kv2[:, sub * MLA_HEAD_PAD + LANES:(sub + 1) * MLA_HEAD_PAD].astype(BF16)

    n_pairs = MLA_HEADS // 2
    pending = project(0)
    for pair in range(n_pairs):
        upcoming = project(pair + 1) if pair + 1 < n_pairs else None
        finish(pair, *pending)
        pending = upcoming


def _mla_proj(x2d, gain, w_in, qn, kvn, wq, wkv, cos, sin, gq, gk, batch, seq):
    t, d = x2d.shape
    tm = MLA_TM
    tps = seq // tm
    hd = MLA_HEADS

    return pl.pallas_call(
        _mla_proj_kernel,
        grid=(t // tm,),
        in_specs=[
            pl.BlockSpec((tm, d), lambda i: (i, 0)),
            _resident(gain), _resident(w_in), _resident(qn), _resident(kvn), _resident(wq), _resident(wkv),
            pl.BlockSpec((tm, LANES), lambda i: (i % tps, 0)),
            pl.BlockSpec((tm, LANES), lambda i: (i % tps, 0)),
            _resident(gq), _resident(gk),
        ],
        out_specs=[
            pl.BlockSpec((1, hd, tm, MLA_HEAD_PAD), lambda i: (i // tps, 0, i % tps, 0)),
            pl.BlockSpec((1, hd, tm, MLA_HEAD_PAD), lambda i: (i // tps, 0, i % tps, 0)),
            pl.BlockSpec((1, hd, tm, MLA_V_DIM), lambda i: (i // tps, 0, i % tps, 0)),
        ],
        out_shape=[
            jax.ShapeDtypeStruct((batch, hd, seq, MLA_HEAD_PAD), BF16),
            jax.ShapeDtypeStruct((batch, hd, seq, MLA_HEAD_PAD), BF16),
            jax.ShapeDtypeStruct((batch, hd, seq, MLA_V_DIM), BF16),
        ],
        compiler_params=_params("parallel"),
        name="mla_proj",
    )(x2d, gain, w_in, qn, kvn, wq, wkv, cos, sin, gq, gk)


def _attn_kernel(q_ref, k_ref, v_ref, o_ref, *, n_q_blocks):
    tq = ATT_TQ
    row = lax.broadcasted_iota(jnp.int32, (tq, tq), 0) // CHUNK
    col = lax.broadcasted_iota(jnp.int32, (tq, tq), 1) // CHUNK
    visible = col <= row

    def scores(hh, qi):
        q0 = qi * tq
        q = q_ref[0, hh, q0:q0 + tq, :]
        s_diag = jnp.where(visible, _dot_nt(q, k_ref[0, hh, q0:q0 + tq, :]), MASK_VALUE)
        s_off = _dot_nt(q, k_ref[0, hh, 0:q0, :]) if qi > 0 else None
        return s_diag, s_off

    def finish(hh, qi, s_diag, s_off):
        q0 = qi * tq
        m = jnp.max(s_diag, axis=-1, keepdims=True)
        if qi > 0:
            m = jnp.maximum(m, jnp.max(s_off, axis=-1, keepdims=True))
        p_diag = jnp.exp2(s_diag - m)
        l = jnp.sum(p_diag, axis=-1, keepdims=True)
        acc = _dot(p_diag.astype(BF16), v_ref[0, hh, q0:q0 + tq, :])
        if qi > 0:
            p_off = jnp.exp2(s_off - m)
            l = l + jnp.sum(p_off, axis=-1, keepdims=True)
            acc = acc + _dot(p_off.astype(BF16), v_ref[0, hh, 0:q0, :])
        o_ref[0, q0:q0 + tq, hh * MLA_V_DIM:(hh + 1) * MLA_V_DIM] = (acc / l).astype(BF16)

    work = [(hh, qi) for hh in range(ATT_HEADS_PER_STEP) for qi in range(n_q_blocks - 1, -1, -1)]
    queue = [scores(*item) for item in work[:ATT_AHEAD]]
    for pos, item in enumerate(work):
        if pos + ATT_AHEAD < len(work):
            queue.append(scores(*work[pos + ATT_AHEAD]))
        finish(*item, *queue.pop(0))


def _attention(q, k, v):
    batch, hd, seq, _ = q.shape
    hps = ATT_HEADS_PER_STEP
    kern = functools.partial(_attn_kernel, n_q_blocks=seq // ATT_TQ)
    return pl.pallas_call(
        kern,
        grid=(batch, hd // hps),
        in_specs=[
            pl.BlockSpec((1, hps, seq, MLA_HEAD_PAD), lambda b, h: (b, h, 0, 0)),
            pl.BlockSpec((1, hps, seq, MLA_HEAD_PAD), lambda b, h: (b, h, 0, 0)),
            pl.BlockSpec((1, hps, seq, MLA_V_DIM), lambda b, h: (b, h, 0, 0)),
        ],
        out_specs=pl.BlockSpec((1, seq, hps * MLA_V_DIM), lambda b, h: (b, 0, h)),
        out_shape=jax.ShapeDtypeStruct((batch, seq, hd * MLA_V_DIM), BF16),
        compiler_params=_params("parallel", "parallel"),
        name="mla_attention",
    )(q, k, v)


def _rope_tables(seq, half):
    inv_freq = ROPE_THETA ** (-jnp.arange(half, dtype=F32) / half)
    ang = jnp.arange(seq).astype(F32)[:, None] * inv_freq[None, :]
    return jnp.cos(ang), jnp.sin(ang)


def _retention_decay_tables():
    sc = RET_SUPER
    log_gamma = jnp.log1p(-jnp.exp2(RET_GAMMA_BASE - jnp.arange(RET_HEADS, dtype=F32)))
    idx = jnp.arange(sc, dtype=F32)
    dist = jnp.abs(idx[:, None] - idx[None, :])
    chunk = jnp.arange(sc) // CHUNK
    visible = chunk[None, :] <= chunk[:, None]
    dmat = jnp.where(visible[None], jnp.exp(log_gamma[:, None, None] * dist[None]), 0.0)
    qd = jnp.exp(log_gamma[:, None] * (idx + 1.0))[:, :, None]
    kd = jnp.exp(log_gamma[:, None] * (sc - 1.0 - idx))[:, :, None]
    return dmat, qd, kd


def _swap_halves(w):
    half = w.shape[-1] // 2
    return jnp.concatenate([w[..., half:], w[..., :half]], axis=-1)


def _pad_lanes(w):
    return jnp.pad(w, [(0, 0)] * (w.ndim - 1) + [(0, LANES - w.shape[-1])])


def _mla_weights(w_in, w_qb, q_head_g, k_head_g):
    rq, rkv = MLA_Q_RANK, MLA_KV_RANK
    w_kr = w_in[:, rq + rkv:]
    w_in_l = jnp.concatenate([w_in[:, :rq + rkv], _pad_lanes(w_kr), _pad_lanes(_swap_halves(w_kr))], axis=1)
    w_qb3 = w_qb.reshape(rq, MLA_HEADS, MLA_QK_DIM)
    nope, rope = w_qb3[..., :MLA_NOPE_DIM], w_qb3[..., MLA_NOPE_DIM:]
    wq = jnp.concatenate([nope, _pad_lanes(rope), _pad_lanes(_swap_halves(rope))], axis=-1)
    wq = wq.reshape(rq, MLA_HEADS * MLA_Q_COLS)

    def gains(g):
        rope_g = g[MLA_NOPE_DIM:]
        return jnp.stack([g[:MLA_NOPE_DIM], _pad_lanes(rope_g), _pad_lanes(_swap_halves(rope_g))])

    return w_in_l, wq, gains(q_head_g), gains(k_head_g)


def kernel(x, ret_norm, ret_w_in, ret_gn, ret_w_out, mla_norm, mla_w_in, mla_q_norm, mla_w_qb, mla_kv_norm,
           mla_w_kvb, mla_q_head_norm, mla_k_head_norm, mla_w_out, ffn_norm, ffn_w_in, ffn_conv_w, ffn_conv_b,
           ffn_w_out):
    batch, seq, d = x.shape
    x2d = x.reshape(batch * seq, d)

    cos_r, sin_r = _rope_tables(seq, RET_QK_DIM // 2)
    dmat, qd, kd = _retention_decay_tables()
    qkvg = _ret_in_proj(x2d, ret_norm[0][None, :], ret_w_in, cos_r, sin_r, seq)
    ret = _ret_core(qkvg, dmat, qd, kd, ret_gn[0][:, None, :], batch, seq)
    x2d = _mix_ffn(ret.reshape(batch * seq, -1), ret_w_out, x2d, ffn_norm[0][None, :],
                   ffn_w_in, ffn_conv_w[0], ffn_conv_b[0][None, :], ffn_w_out, seq, layer=0)

    cos_m, sin_m = _rope_tables(seq, MLA_ROPE_DIM // 2)
    pad = jnp.zeros((seq, LANES - MLA_ROPE_DIM), F32)
    cos_t = jnp.concatenate([cos_m, cos_m, pad], axis=1)
    sin_t = jnp.concatenate([-sin_m, sin_m, pad], axis=1)
    w_in_l, wq, gq, gk = _mla_weights(mla_w_in[0], mla_w_qb[0], mla_q_head_norm[0], mla_k_head_norm[0])
    q, k, v = _mla_proj(x2d, mla_norm[0][None, :], w_in_l.astype(BF16), mla_q_norm[0][None, :],
                        mla_kv_norm[0][None, :], wq.astype(BF16), mla_w_kvb[0].astype(BF16),
                        cos_t, sin_t, gq, gk, batch, seq)
    att = _attention(q, k, v)
    x2d = _mix_ffn(att.reshape(batch * seq, -1), mla_w_out, x2d, ffn_norm[1][None, :],
                   ffn_w_in, ffn_conv_w[1], ffn_conv_b[1][None, :], ffn_w_out, seq, layer=1)
    return x2d.reshape(batch, seq, d)
```

```python
import functools

import jax
import jax.numpy as jnp
from jax import lax
from jax.experimental import pallas as pl
from jax.experimental.pallas import tpu as pltpu

F32 = jnp.float32
BF16 = jnp.bfloat16

CHUNK = 64
RMS_EPS = 1e-6
ROPE_THETA = 10000.0
RET_HEADS = 4
RET_QK_DIM = 256
RET_V_DIM = 512
RET_GAMMA_BASE = -5.0
MLA_HEADS = 8
MLA_Q_RANK = 384
MLA_KV_RANK = 256
MLA_NOPE_DIM = 128
MLA_ROPE_DIM = 64
MLA_V_DIM = 128
MLA_QK_DIM = MLA_NOPE_DIM + MLA_ROPE_DIM
MLA_HEAD_PAD = 256
MLA_Q_COLS = 3 * 128
MASK_VALUE = -1e30
CONV_WIDTH = 3
LOG2_E = 1.4426950408889634

LANES = 128
SUBLANES = 8
VMEM_LIMIT_BYTES = 52 * 1024 * 1024
BF16_ROWS = 16

RET_SUPER = 256
PROJ_TM = 512
RET_HEADS_PER_STEP = 2
FFN_TM = 512
FFN_CHUNK = 256
MLA_TM = 512
W_STAGE_BYTES = 384 * 1024
W_STAGE_SLOTS = 4
ATT_TQ = 256
ATT_HEADS_PER_STEP = 2
ATT_AHEAD = 2


def _params(*sem):
    return pltpu.CompilerParams(dimension_semantics=sem, vmem_limit_bytes=VMEM_LIMIT_BYTES)


def _resident(arr):
    return pl.BlockSpec(arr.shape, lambda i: (0,) * arr.ndim, pipeline_mode=pl.Buffered(1))


def _stage_rows(rows, cols):
    limit = W_STAGE_BYTES // (cols * 4)
    fits = [r for r in range(BF16_ROWS, limit + 1, BF16_ROWS) if rows % r == 0]
    return max(fits)


def _weight_scratch(rows, cols):
    return [pltpu.VMEM((rows, cols), BF16), pltpu.VMEM((W_STAGE_SLOTS, _stage_rows(rows, cols), cols), F32),
            pltpu.SemaphoreType.DMA((W_STAGE_SLOTS,))]


def _load_weight(src_ref, dst_ref, stage_ref, sem_ref):
    slots, rows = stage_ref.shape[0], stage_ref.shape[1]
    n_chunks = dst_ref.shape[0] // rows
    lookahead = slots - 1
    assert n_chunks >= lookahead

    def copy(k, slot):
        return pltpu.make_async_copy(src_ref.at[pl.ds(k * rows, rows), :], stage_ref.at[slot], sem_ref.at[slot])

    for k in range(lookahead):
        copy(k, k).start()

    def body(k, carry):
        nxt = k + lookahead

        @pl.when(nxt < n_chunks)
        def _():
            copy(nxt, lax.rem(nxt, slots)).start()

        slot = lax.rem(k, slots)
        copy(k, slot).wait()
        r0 = pl.multiple_of(k * rows, rows)
        dst_ref[pl.ds(r0, rows), :] = stage_ref[slot].astype(BF16)
        return carry

    lax.fori_loop(0, n_chunks, body, 0)


def _rms_scale(x):
    return lax.rsqrt(jnp.mean(x * x, axis=-1, keepdims=True) + RMS_EPS)


def _silu(x):
    half = 0.5 * x
    return half + half * jnp.tanh(half)


def _dot(a, b):
    return jnp.dot(a, b, preferred_element_type=F32)


def _dot_nt(a, b):
    return lax.dot_general(a, b, (((1,), (1,)), ((), ())), preferred_element_type=F32)


def _dot_tn(a, b):
    return lax.dot_general(a, b, (((0,), (0,)), ((), ())), preferred_element_type=F32)


def _ret_in_kernel(x_ref, gain_ref, w_hbm, cos_ref, sin_ref, o_ref, h_ref, w_ref, stage_ref, sem_ref):
    half = RET_QK_DIM // 2

    @pl.when(pl.program_id(0) == 0)
    def _():
        _load_weight(w_hbm.at[0], w_ref, stage_ref, sem_ref)

    x = x_ref[...]
    h_ref[...] = (x * _rms_scale(x) * gain_ref[...]).astype(BF16)
    cos = cos_ref[...]
    sin = sin_ref[...]
    k_scale = RET_QK_DIM ** -0.5
    cos_k = cos * k_scale
    sin_k = sin * k_scale
    for c in range(w_ref.shape[1] // RET_QK_DIM):
        lo = c * RET_QK_DIM
        acc = _dot(h_ref[...], w_ref[:, lo:lo + RET_QK_DIM])
        if c < 2 * RET_HEADS:
            cs, sn = (cos, sin) if c < RET_HEADS else (cos_k, sin_k)
            x1 = acc[:, :half]
            x2 = acc[:, half:]
            o_ref[:, lo:lo + half] = (x1 * cs - x2 * sn).astype(BF16)
            o_ref[:, lo + half:lo + RET_QK_DIM] = (x2 * cs + x1 * sn).astype(BF16)
        else:
            o_ref[:, lo:lo + RET_QK_DIM] = acc.astype(BF16)


def _ret_in_proj(x2d, gain, w, cos, sin, seq):
    t, d = x2d.shape
    n = w.shape[-1]
    tm = PROJ_TM
    tps = seq // tm
    return pl.pallas_call(
        _ret_in_kernel,
        grid=(t // tm,),
        in_specs=[
            pl.BlockSpec((tm, d), lambda i: (i, 0)),
            _resident(gain), pl.BlockSpec(memory_space=pl.ANY),
            pl.BlockSpec((tm, RET_QK_DIM // 2), lambda i: (i % tps, 0)),
            pl.BlockSpec((tm, RET_QK_DIM // 2), lambda i: (i % tps, 0)),
        ],
        out_specs=pl.BlockSpec((tm, n), lambda i: (i, 0)),
        out_shape=jax.ShapeDtypeStruct((t, n), BF16),
        scratch_shapes=[pltpu.VMEM((tm, d), BF16)] + _weight_scratch(d, n),
        compiler_params=_params("arbitrary"),
        name="ret_in_proj",
    )(x2d, gain, w, cos, sin)


def _ret_core_kernel(q_ref, k_ref, v_ref, g_ref, dmat_ref, qd_ref, kd_ref, gn_ref, o_ref, state_ref, *, n_steps):
    dk, dv = RET_QK_DIM, RET_V_DIM
    state_ref[...] = jnp.zeros_like(state_ref)

    for sc in range(n_steps):
        rows = slice(sc * RET_SUPER, (sc + 1) * RET_SUPER)
        for hh in range(RET_HEADS_PER_STEP):
            qd = qd_ref[hh]
            kd = kd_ref[hh]
            step_decay = qd[RET_SUPER - 1:RET_SUPER, :]
            q = q_ref[0, rows, hh * dk:(hh + 1) * dk]
            k = k_ref[0, rows, hh * dk:(hh + 1) * dk]
            v = v_ref[0, rows, hh * dv:(hh + 1) * dv]
            scores = _dot_nt(q, k) * dmat_ref[hh]
            inner = _dot(scores.astype(BF16), v)
            state = state_ref[hh]
            q_scaled = (q.astype(F32) * qd).astype(BF16)
            cross = _dot(q_scaled, state.astype(BF16))
            k_scaled = (k.astype(F32) * kd).astype(BF16)
            state_ref[hh] = state * step_decay + _dot_tn(k_scaled, v)
            out = inner + cross
            out = out * _rms_scale(out) * gn_ref[hh]
            g = g_ref[0, rows, hh * dv:(hh + 1) * dv].astype(F32)
            o_ref[0, rows, hh * dv:(hh + 1) * dv] = (out * _silu(g)).astype(BF16)


def _ret_core(qkvg, dmat, qd, kd, gn, batch, seq):
    hd, dk, dv = RET_HEADS, RET_QK_DIM, RET_V_DIM
    hps = RET_HEADS_PER_STEP
    groups = hd // hps
    qkvg3 = qkvg.reshape(batch, seq, qkvg.shape[-1])
    k_blk0 = groups
    v_blk0 = 2 * hd * dk // (hps * dv)
    g_blk0 = v_blk0 + groups
    kern = functools.partial(_ret_core_kernel, n_steps=seq // RET_SUPER)
    return pl.pallas_call(
        kern,
        grid=(batch, groups),
        in_specs=[
            pl.BlockSpec((1, seq, hps * dk), lambda b, h: (b, 0, h)),
            pl.BlockSpec((1, seq, hps * dk), lambda b, h: (b, 0, k_blk0 + h)),
            pl.BlockSpec((1, seq, hps * dv), lambda b, h: (b, 0, v_blk0 + h)),
            pl.BlockSpec((1, seq, hps * dv), lambda b, h: (b, 0, g_blk0 + h)),
            pl.BlockSpec((hps, RET_SUPER, RET_SUPER), lambda b, h: (h, 0, 0)),
            pl.BlockSpec((hps, RET_SUPER, 1), lambda b, h: (h, 0, 0)),
            pl.BlockSpec((hps, RET_SUPER, 1), lambda b, h: (h, 0, 0)),
            pl.BlockSpec((hps, 1, dv), lambda b, h: (h, 0, 0)),
        ],
        out_specs=pl.BlockSpec((1, seq, hps * dv), lambda b, h: (b, 0, h)),
        out_shape=jax.ShapeDtypeStruct((batch, seq, hd * dv), BF16),
        scratch_shapes=[pltpu.VMEM((hps, dk, dv), F32)],
        compiler_params=_params("parallel", "parallel"),
        name="ret_core",
    )(qkvg3, qkvg3, qkvg3, qkvg3, dmat, qd, kd, gn)


def _mix_ffn_kernel(a_ref, wo_hbm, res_ref, gain_ref, w_in_hbm, cw_ref, cb_ref, w_out_hbm, o_ref,
                    h_ref, act_ref, carry_ref, wo_ref, wo_stage, wo_sem, w_in_ref, w_in_stage, w_in_sem,
                    w_out_ref, w_out_stage, w_out_sem, *, tiles_per_seq, layer):
    i = pl.program_id(0)
    tm = res_ref.shape[0]
    ffn_dim = w_out_ref.shape[0]

    @pl.when(i == 0)
    def _():
        _load_weight(wo_hbm.at[0], wo_ref, wo_stage, wo_sem)
        _load_weight(w_in_hbm.at[layer], w_in_ref, w_in_stage, w_in_sem)
        _load_weight(w_out_hbm.at[layer], w_out_ref, w_out_stage, w_out_sem)

    x1 = res_ref[...] + _dot(a_ref[...], wo_ref[...])
    o_ref[...] = x1
    h_ref[...] = (x1 * _rms_scale(x1) * gain_ref[...]).astype(BF16)

    seq_start = lax.rem(i, tiles_per_seq) == 0
    row = lax.broadcasted_iota(jnp.int32, (tm, 1), 0)
    for c in range(ffn_dim // FFN_CHUNK):
        lo = c * FFN_CHUNK
        a = _dot(h_ref[...], w_in_ref[:, lo:lo + FFN_CHUNK])
        g = _dot(h_ref[...], w_in_ref[:, ffn_dim + lo:ffn_dim + lo + FFN_CHUNK])
        prev = jnp.where(seq_start, 0.0, carry_ref[:, lo:lo + FFN_CHUNK])
        prev1 = prev[SUBLANES - 1:SUBLANES, :]
        prev2 = prev[SUBLANES - 2:SUBLANES - 1, :]
        g1 = jnp.where(row == 0, prev1, pltpu.roll(g, 1, 0))
        g2 = jnp.where(row == 0, prev2, jnp.where(row == 1, prev1, pltpu.roll(g, 2, 0)))
        carry_ref[:, lo:lo + FFN_CHUNK] = g[tm - SUBLANES:tm, :]
        cw = cw_ref[:, lo:lo + FFN_CHUNK]
        gc = g2 * cw[0:1, :] + g1 * cw[1:2, :] + g * cw[2:3, :] + cb_ref[:, lo:lo + FFN_CHUNK]
        act_ref[:, lo:lo + FFN_CHUNK] = (_silu(gc) * a).astype(BF16)

    o_ref[...] += _dot(act_ref[...], w_out_ref[...])


def _mix_ffn(a, w_o, res, gain, w_in, conv_w, conv_b, w_out, seq, layer):
    t, d = res.shape
    ka = a.shape[1]
    f = w_out.shape[1]
    tm = FFN_TM
    hbm = pl.BlockSpec(memory_space=pl.ANY)
    kern = functools.partial(_mix_ffn_kernel, tiles_per_seq=seq // tm, layer=layer)
    return pl.pallas_call(
        kern,
        grid=(t // tm,),
        in_specs=[
            pl.BlockSpec((tm, ka), lambda i: (i, 0)),
            hbm,
            pl.BlockSpec((tm, d), lambda i: (i, 0)),
            _resident(gain), hbm, _resident(conv_w), _resident(conv_b), hbm,
        ],
        out_specs=pl.BlockSpec((tm, d), lambda i: (i, 0)),
        out_shape=jax.ShapeDtypeStruct((t, d), F32),
        scratch_shapes=([pltpu.VMEM((tm, d), BF16), pltpu.VMEM((tm, f), BF16), pltpu.VMEM((SUBLANES, f), F32)]
                        + _weight_scratch(ka, d) + _weight_scratch(d, 2 * f) + _weight_scratch(f, d)),
        compiler_params=_params("arbitrary"),
        name="mix_ffn",
    )(a, w_o, res, gain, w_in, conv_w, conv_b, w_out)


def _mla_proj_kernel(x_ref, gain_ref, w_in_ref, qn_ref, kvn_ref, wq_ref, wkv_ref,
                     c_ref, s_ref, gq_ref, gk_ref, q_ref, k_ref, v_ref):
    x = x_ref[...]
    h = (x * _rms_scale(x) * gain_ref[...]).astype(BF16)
    p = _dot(h, w_in_ref[...])
    c_q = p[:, :MLA_Q_RANK]
    c_kv = p[:, MLA_Q_RANK:MLA_Q_RANK + MLA_KV_RANK]
    k_r = p[:, MLA_Q_RANK + MLA_KV_RANK:MLA_Q_RANK + MLA_KV_RANK + LANES]
    k_s = p[:, MLA_Q_RANK + MLA_KV_RANK + LANES:]
    c_q = (c_q * _rms_scale(c_q) * qn_ref[...]).astype(BF16)
    c_kv = (c_kv * _rms_scale(c_kv) * kvn_ref[...]).astype(BF16)

    cos = c_ref[...]
    sin = s_ref[...]
    gq_nope, gq_rope, gq_swap = gq_ref[0:1, :], gq_ref[1:2, :], gq_ref[2:3, :]
    gk_nope, gk_rope, gk_swap = gk_ref[0:1, :], gk_ref[1:2, :], gk_ref[2:3, :]
    inv_dim = 1.0 / MLA_QK_DIM
    q_scale = MLA_QK_DIM ** -0.5 * LOG2_E

    k_rope_sq = k_r * k_r
    k_rope_rot = k_r * gk_rope * cos + k_s * gk_swap * sin

    def project(pair):
        q2 = _dot(c_q, wq_ref[:, pair * 2 * MLA_Q_COLS:(pair + 1) * 2 * MLA_Q_COLS])
        kv2 = _dot(c_kv, wkv_ref[:, pair * 2 * MLA_HEAD_PAD:(pair + 1) * 2 * MLA_HEAD_PAD])
        return q2, kv2

    def finish(pair, q2, kv2):
        for sub in range(2):
            hh = 2 * pair + sub
            q_n = q2[:, sub * MLA_Q_COLS:sub * MLA_Q_COLS + LANES]
            q_r = q2[:, sub * MLA_Q_COLS + LANES:sub * MLA_Q_COLS + 2 * LANES]
            q_s = q2[:, sub * MLA_Q_COLS + 2 * LANES:(sub + 1) * MLA_Q_COLS]
            ss = jnp.sum(q_n * q_n + q_r * q_r, axis=-1, keepdims=True)
            r = lax.rsqrt(ss * inv_dim + RMS_EPS) * q_scale
            q_ref[0, hh, :, 0:LANES] = (q_n * r * gq_nope).astype(BF16)
            q_ref[0, hh, :, LANES:2 * LANES] = ((q_r * gq_rope * cos + q_s * gq_swap * sin) * r).astype(BF16)

            k_n = kv2[:, sub * MLA_HEAD_PAD:sub * MLA_HEAD_PAD + LANES]
            ssk = jnp.sum(k_n * k_n + k_rope_sq, axis=-1, keepdims=True)
            rk = lax.rsqrt(ssk * inv_dim + RMS_EPS)
            k_ref[0, hh, :, 0:LANES] = (k_n * rk * gk_nope).astype(BF16)
            k_ref[0, hh, :, LANES:2 * LANES] = (k_rope_rot * rk).astype(BF16)
            v_ref[0, hh] = kv2[:, sub * MLA_HEAD_PAD + LANES:(sub + 1) * MLA_HEAD_PAD].astype(BF16)

    n_pairs = MLA_HEADS // 2
    pending = project(0)
    for pair in range(n_pairs):
        upcoming = project(pair + 1) if pair + 1 < n_pairs else None
        finish(pair, *pending)
        pending = upcoming


def _mla_proj(x2d, gain, w_in, qn, kvn, wq, wkv, cos, sin, gq, gk, batch, seq):
    t, d = x2d.shape
    tm = MLA_TM
    tps = seq // tm
    hd = MLA_HEADS

    return pl.pallas_call(
        _mla_proj_kernel,
        grid=(t // tm,),
        in_specs=[
            pl.BlockSpec((tm, d), lambda i: (i, 0)),
            _resident(gain), _resident(w_in), _resident(qn), _resident(kvn), _resident(wq), _resident(wkv),
            pl.BlockSpec((tm, LANES), lambda i: (i % tps, 0)),
            pl.BlockSpec((tm, LANES), lambda i: (i % tps, 0)),
            _resident(gq), _resident(gk),
        ],
        out_specs=[
            pl.BlockSpec((1, hd, tm, MLA_HEAD_PAD), lambda i: (i // tps, 0, i % tps, 0)),
            pl.BlockSpec((1, hd, tm, MLA_HEAD_PAD), lambda i: (i // tps, 0, i % tps, 0)),
            pl.BlockSpec((1, hd, tm, MLA_V_DIM), lambda i: (i // tps, 0, i % tps, 0)),
        ],
        out_shape=[
            jax.ShapeDtypeStruct((batch, hd, seq, MLA_HEAD_PAD), BF16),
            jax.ShapeDtypeStruct((batch, hd, seq, MLA_HEAD_PAD), BF16),
            jax.ShapeDtypeStruct((batch, hd, seq, MLA_V_DIM), BF16),
        ],
        compiler_params=_params("parallel"),
        name="mla_proj",
    )(x2d, gain, w_in, qn, kvn, wq, wkv, cos, sin, gq, gk)


def _attn_kernel(q_ref, k_ref, v_ref, o_ref, *, n_q_blocks):
    tq = ATT_TQ
    row = lax.broadcasted_iota(jnp.int32, (tq, tq), 0) // CHUNK
    col = lax.broadcasted_iota(jnp.int32, (tq, tq), 1) // CHUNK
    visible = col <= row

    def scores(hh, qi):
        q0 = qi * tq
        q = q_ref[0, hh, q0:q0 + tq, :]
        s_diag = jnp.where(visible, _dot_nt(q, k_ref[0, hh, q0:q0 + tq, :]), MASK_VALUE)
        s_off = _dot_nt(q, k_ref[0, hh, 0:q0, :]) if qi > 0 else None
        return s_diag, s_off

    def finish(hh, qi, s_diag, s_off):
        q0 = qi * tq
        m = jnp.max(s_diag, axis=-1, keepdims=True)
        if qi > 0:
            m = jnp.maximum(m, jnp.max(s_off, axis=-1, keepdims=True))
        p_diag = jnp.exp2(s_diag - m)
        l = jnp.sum(p_diag, axis=-1, keepdims=True)
        acc = _dot(p_diag.astype(BF16), v_ref[0, hh, q0:q0 + tq, :])
        if qi > 0:
            p_off = jnp.exp2(s_off - m)
            l = l + jnp.sum(p_off, axis=-1, keepdims=True)
            acc = acc + _dot(p_off.astype(BF16), v_ref[0, hh, 0:q0, :])
        o_ref[0, q0:q0 + tq, hh * MLA_V_DIM:(hh + 1) * MLA_V_DIM] = (acc / l).astype(BF16)

    work = [(hh, qi) for hh in range(ATT_HEADS_PER_STEP) for qi in range(n_q_blocks - 1, -1, -1)]
    queue = [scores(*item) for item in work[:ATT_AHEAD]]
    for pos, item in enumerate(work):
        if pos + ATT_AHEAD < len(work):
            queue.append(scores(*work[pos + ATT_AHEAD]))
        finish(*item, *queue.pop(0))


def _attention(q, k, v):
    batch, hd, seq, _ = q.shape
    hps = ATT_HEADS_PER_STEP
    kern = functools.partial(_attn_kernel, n_q_blocks=seq // ATT_TQ)
    return pl.pallas_call(
        kern,
        grid=(batch, hd // hps),
        in_specs=[
            pl.BlockSpec((1, hps, seq, MLA_HEAD_PAD), lambda b, h: (b, h, 0, 0)),
            pl.BlockSpec((1, hps, seq, MLA_HEAD_PAD), lambda b, h: (b, h, 0, 0)),
            pl.BlockSpec((1, hps, seq, MLA_V_DIM), lambda b, h: (b, h, 0, 0)),
        ],
        out_specs=pl.BlockSpec((1, seq, hps * MLA_V_DIM), lambda b, h: (b, 0, h)),
        out_shape=jax.ShapeDtypeStruct((batch, seq, hd * MLA_V_DIM), BF16),
        compiler_params=_params("parallel", "parallel"),
        name="mla_attention",
    )(q, k, v)


def _rope_tables(seq, half):
    inv_freq = ROPE_THETA ** (-jnp.arange(half, dtype=F32) / half)
    ang = jnp.arange(seq).astype(F32)[:, None] * inv_freq[None, :]
    return jnp.cos(ang), jnp.sin(ang)


def _retention_decay_tables():
    sc = RET_SUPER
    log_gamma = jnp.log1p(-jnp.exp2(RET_GAMMA_BASE - jnp.arange(RET_HEADS, dtype=F32)))
    idx = jnp.arange(sc, dtype=F32)
    dist = jnp.abs(idx[:, None] - idx[None, :])
    chunk = jnp.arange(sc) // CHUNK
    visible = chunk[None, :] <= chunk[:, None]
    dmat = jnp.where(visible[None], jnp.exp(log_gamma[:, None, None] * dist[None]), 0.0)
    qd = jnp.exp(log_gamma[:, None] * (idx + 1.0))[:, :, None]
    kd = jnp.exp(log_gamma[:, None] * (sc - 1.0 - idx))[:, :, None]
    return dmat, qd, kd


def _swap_halves(w):
    half = w.shape[-1] // 2
    return jnp.concatenate([w[..., half:], w[..., :half]], axis=-1)


def _pad_lanes(w):
    return jnp.pad(w, [(0, 0)] * (w.ndim - 1) + [(0, LANES - w.shape[-1])])


def _mla_weights(w_in, w_qb, q_head_g, k_head_g):
    rq, rkv = MLA_Q_RANK, MLA_KV_RANK
    w_kr = w_in[:, rq + rkv:]
    w_in_l = jnp.concatenate([w_in[:, :rq + rkv], _pad_lanes(w_kr), _pad_lanes(_swap_halves(w_kr))], axis=1)
    w_qb3 = w_qb.reshape(rq, MLA_HEADS, MLA_QK_DIM)
    nope, rope = w_qb3[..., :MLA_NOPE_DIM], w_qb3[..., MLA_NOPE_DIM:]
    wq = jnp.concatenate([nope, _pad_lanes(rope), _pad_lanes(_swap_halves(rope))], axis=-1)
    wq = wq.reshape(rq, MLA_HEADS * MLA_Q_COLS)

    def gains(g):
        rope_g = g[MLA_NOPE_DIM:]
        return jnp.stack([g[:MLA_NOPE_DIM], _pad_lanes(rope_g), _pad_lanes(_swap_halves(rope_g))])

    return w_in_l, wq, gains(q_head_g), gains(k_head_g)


def kernel(x, ret_norm, ret_w_in, ret_gn, ret_w_out, mla_norm, mla_w_in, mla_q_norm, mla_w_qb, mla_kv_norm,
           mla_w_kvb, mla_q_head_norm, mla_k_head_norm, mla_w_out, ffn_norm, ffn_w_in, ffn_conv_w, ffn_conv_b,
           ffn_w_out):
    batch, seq, d = x.shape
    x2d = x.reshape(batch * seq, d)

    cos_r, sin_r = _rope_tables(seq, RET_QK_DIM // 2)
    dmat, qd, kd = _retention_decay_tables()
    qkvg = _ret_in_proj(x2d, ret_norm[0][None, :], ret_w_in, cos_r, sin_r, seq)
    ret = _ret_core(qkvg, dmat, qd, kd, ret_gn[0][:, None, :], batch, seq)
    x2d = _mix_ffn(ret.reshape(batch * seq, -1), ret_w_out, x2d, ffn_norm[0][None, :],
                   ffn_w_in, ffn_conv_w[0], ffn_conv_b[0][None, :], ffn_w_out, seq, layer=0)

    cos_m, sin_m = _rope_tables(seq, MLA_ROPE_DIM // 2)
    pad = jnp.zeros((seq, LANES - MLA_ROPE_DIM), F32)
    cos_t = jnp.concatenate([cos_m, cos_m, pad], axis=1)
    sin_t = jnp.concatenate([-sin_m, sin_m, pad], axis=1)
    w_in_l, wq, gq, gk = _mla_weights(mla_w_in[0], mla_w_qb[0], mla_q_head_norm[0], mla_k_head_norm[0])
    q, k, v = _mla_proj(x2d, mla_norm[0][None, :], w_in_l.astype(BF16), mla_q_norm[0][None, :],
                        mla_kv_norm[0][None, :], wq.astype(BF16), mla_w_kvb[0].astype(BF16),
                        cos_t, sin_t, gq, gk, batch, seq)
    att = _attention(q, k, v)
    x2d = _mix_ffn(att.reshape(batch * seq, -1), mla_w_out, x2d, ffn_norm[1][None, :],
                   ffn_w_in, ffn_conv_w[1], ffn_conv_b[1][None, :], ffn_w_out, seq, layer=1)
    return x2d.reshape(batch, seq, d)
```

```python
import functools

import jax
import jax.numpy as jnp
from jax import lax
from jax.experimental import pallas as pl
from jax.experimental.pallas import tpu as pltpu

F32 = jnp.float32
BF16 = jnp.bfloat16

CHUNK = 64
RMS_EPS = 1e-6
ROPE_THETA = 10000.0
RET_HEADS = 4
RET_QK_DIM = 256
RET_V_DIM = 512
RET_GAMMA_BASE = -5.0
MLA_HEADS = 8
MLA_Q_RANK = 384
MLA_KV_RANK = 256
MLA_NOPE_DIM = 128
MLA_ROPE_DIM = 64
MLA_V_DIM = 128
MLA_QK_DIM = MLA_NOPE_DIM + MLA_ROPE_DIM
MLA_HEAD_PAD = 256
MLA_Q_COLS = 3 * 128
MASK_VALUE = -1e30
CONV_WIDTH = 3
LOG2_E = 1.4426950408889634

LANES = 128
SUBLANES = 8
VMEM_LIMIT_BYTES = 52 * 1024 * 1024
BF16_ROWS = 16

RET_SUPER = 256
PROJ_TM = 512
RET_HEADS_PER_STEP = 2
FFN_TM = 512
FFN_CHUNK = 256
MLA_TM = 512
W_STAGE_BYTES = 384 * 1024
W_STAGE_SLOTS = 4
ATT_TQ = 256
ATT_HEADS_PER_STEP = 2
ATT_AHEAD = 2


def _params(*sem):
    return pltpu.CompilerParams(dimension_semantics=sem, vmem_limit_bytes=VMEM_LIMIT_BYTES)


def _resident(arr):
    return pl.BlockSpec(arr.shape, lambda i: (0,) * arr.ndim, pipeline_mode=pl.Buffered(1))


def _stage_rows(rows, cols):
    limit = W_STAGE_BYTES // (cols * 4)
    fits = [r for r in range(BF16_ROWS, limit + 1, BF16_ROWS) if rows % r == 0]
    return max(fits)


def _weight_scratch(rows, cols):
    return [pltpu.VMEM((rows, cols), BF16), pltpu.VMEM((W_STAGE_SLOTS, _stage_rows(rows, cols), cols), F32),
            pltpu.SemaphoreType.DMA((W_STAGE_SLOTS,))]


def _load_weight(src_ref, dst_ref, stage_ref, sem_ref):
    slots, rows = stage_ref.shape[0], stage_ref.shape[1]
    n_chunks = dst_ref.shape[0] // rows
    lookahead = slots - 1
    assert n_chunks >= lookahead

    def copy(k, slot):
        return pltpu.make_async_copy(src_ref.at[pl.ds(k * rows, rows), :], stage_ref.at[slot], sem_ref.at[slot])

    for k in range(lookahead):
        copy(k, k).start()

    def body(k, carry):
        nxt = k + lookahead

        @pl.when(nxt < n_chunks)
        def _():
            copy(nxt, lax.rem(nxt, slots)).start()

        slot = lax.rem(k, slots)
        copy(k, slot).wait()
        r0 = pl.multiple_of(k * rows, rows)
        dst_ref[pl.ds(r0, rows), :] = stage_ref[slot].astype(BF16)
        return carry

    lax.fori_loop(0, n_chunks, body, 0)


def _rms_scale(x):
    return lax.rsqrt(jnp.mean(x * x, axis=-1, keepdims=True) + RMS_EPS)


def _silu(x):
    half = 0.5 * x
    return half + half * jnp.tanh(half)


def _dot(a, b):
    return jnp.dot(a, b, preferred_element_type=F32)


def _dot_nt(a, b):
    return lax.dot_general(a, b, (((1,), (1,)), ((), ())), preferred_element_type=F32)


def _dot_tn(a, b):
    return lax.dot_general(a, b, (((0,), (0,)), ((), ())), preferred_element_type=F32)


def _ret_in_kernel(*refs, n_side):
    x_ref, gain_ref, w_hbm, cos_ref, sin_ref = refs[:5]
    side_in = refs[5:5 + n_side]
    o_ref = refs[5 + n_side]
    side_out = refs[6 + n_side:6 + 2 * n_side]
    h_ref, w_ref, stage_ref, sem_ref = refs[6 + 2 * n_side:]
    i = pl.program_id(0)
    half = RET_QK_DIM // 2

    @pl.when(i == 0)
    def _():
        _load_weight(w_hbm.at[0], w_ref, stage_ref, sem_ref)

    for src_ref, dst_ref in zip(side_in, side_out):
        dst_ref[...] = src_ref[...].astype(BF16)

    x = x_ref[...]
    h_ref[...] = (x * _rms_scale(x) * gain_ref[...]).astype(BF16)
    cos = cos_ref[...]
    sin = sin_ref[...]
    k_scale = RET_QK_DIM ** -0.5
    cos_k = cos * k_scale
    sin_k = sin * k_scale
    for c in range(w_ref.shape[1] // RET_QK_DIM):
        lo = c * RET_QK_DIM
        acc = _dot(h_ref[...], w_ref[:, lo:lo + RET_QK_DIM])
        if c < 2 * RET_HEADS:
            cs, sn = (cos, sin) if c < RET_HEADS else (cos_k, sin_k)
            x1 = acc[:, :half]
            x2 = acc[:, half:]
            o_ref[:, lo:lo + half] = (x1 * cs - x2 * sn).astype(BF16)
            o_ref[:, lo + half:lo + RET_QK_DIM] = (x2 * cs + x1 * sn).astype(BF16)
        else:
            o_ref[:, lo:lo + RET_QK_DIM] = acc.astype(BF16)


def _cast_plan(arr, n_grid):
    cols = arr.shape[-1]
    rows = arr.size // cols
    steps = max(s for s in range(1, n_grid + 1) if rows % s == 0 and (rows // s) % BF16_ROWS == 0)
    return arr.reshape(rows, cols), steps, rows // steps


def _ret_in_proj(x2d, gain, w, cos, sin, seq, side_weights):
    t, d = x2d.shape
    n = w.shape[-1]
    tm = PROJ_TM
    tps = seq // tm
    n_grid = t // tm
    plans = [_cast_plan(arr, n_grid) for arr in side_weights]
    side_specs = [pl.BlockSpec((rps, view.shape[1]), lambda i, last=steps - 1: (jnp.minimum(i, last), 0))
                  for view, steps, rps in plans]
    kern = functools.partial(_ret_in_kernel, n_side=len(plans))
    outs = pl.pallas_call(
        kern,
        grid=(n_grid,),
        in_specs=[
            pl.BlockSpec((tm, d), lambda i: (i, 0)),
            _resident(gain), pl.BlockSpec(memory_space=pl.ANY),
            pl.BlockSpec((tm, RET_QK_DIM // 2), lambda i: (i % tps, 0)),
            pl.BlockSpec((tm, RET_QK_DIM // 2), lambda i: (i % tps, 0)),
        ] + side_specs,
        out_specs=[pl.BlockSpec((tm, n), lambda i: (i, 0))] + side_specs,
        out_shape=[jax.ShapeDtypeStruct((t, n), BF16)]
        + [jax.ShapeDtypeStruct(view.shape, BF16) for view, _, _ in plans],
        scratch_shapes=[pltpu.VMEM((tm, d), BF16)] + _weight_scratch(d, n),
        compiler_params=_params("arbitrary"),
        name="ret_in_proj",
    )(x2d, gain, w, cos, sin, *[view for view, _, _ in plans])
    return outs[0], [o.reshape(arr.shape) for o, arr in zip(outs[1:], side_weights)]


def _ret_core_kernel(q_ref, k_ref, v_ref, g_ref, dmat_ref, qd_ref, kd_ref, gn_ref, o_ref, state_ref, *, n_steps):
    dk, dv = RET_QK_DIM, RET_V_DIM
    state_ref[...] = jnp.zeros_like(state_ref)

    for sc in range(n_steps):
        rows = slice(sc * RET_SUPER, (sc + 1) * RET_SUPER)
        for hh in range(RET_HEADS_PER_STEP):
            qd = qd_ref[hh]
            kd = kd_ref[hh]
            step_decay = qd[RET_SUPER - 1:RET_SUPER, :]
            q = q_ref[0, rows, hh * dk:(hh + 1) * dk]
            k = k_ref[0, rows, hh * dk:(hh + 1) * dk]
            v = v_ref[0, rows, hh * dv:(hh + 1) * dv]
            scores = _dot_nt(q, k) * dmat_ref[hh]
            inner = _dot(scores.astype(BF16), v)
            state = state_ref[hh]
            q_scaled = (q.astype(F32) * qd).astype(BF16)
            cross = _dot(q_scaled, state.astype(BF16))
            k_scaled = (k.astype(F32) * kd).astype(BF16)
            state_ref[hh] = state * step_decay + _dot_tn(k_scaled, v)
            out = inner + cross
            out = out * _rms_scale(out) * gn_ref[hh]
            g = g_ref[0, rows, hh * dv:(hh + 1) * dv]
            o_ref[0, rows, hh * dv:(hh + 1) * dv] = out.astype(BF16) * _silu(g)


def _ret_core(qkvg, dmat, qd, kd, gn, batch, seq):
    hd, dk, dv = RET_HEADS, RET_QK_DIM, RET_V_DIM
    hps = RET_HEADS_PER_STEP
    groups = hd // hps
    qkvg3 = qkvg.reshape(batch, seq, qkvg.shape[-1])
    k_blk0 = groups
    v_blk0 = 2 * hd * dk // (hps * dv)
    g_blk0 = v_blk0 + groups
    kern = functools.partial(_ret_core_kernel, n_steps=seq // RET_SUPER)
    return pl.pallas_call(
        kern,
        grid=(batch, groups),
        in_specs=[
            pl.BlockSpec((1, seq, hps * dk), lambda b, h: (b, 0, h)),
            pl.BlockSpec((1, seq, hps * dk), lambda b, h: (b, 0, k_blk0 + h)),
            pl.BlockSpec((1, seq, hps * dv), lambda b, h: (b, 0, v_blk0 + h)),
            pl.BlockSpec((1, seq, hps * dv), lambda b, h: (b, 0, g_blk0 + h)),
            pl.BlockSpec((hps, RET_SUPER, RET_SUPER), lambda b, h: (h, 0, 0)),
            pl.BlockSpec((hps, RET_SUPER, 1), lambda b, h: (h, 0, 0)),
            pl.BlockSpec((hps, RET_SUPER, 1), lambda b, h: (h, 0, 0)),
            pl.BlockSpec((hps, 1, dv), lambda b, h: (h, 0, 0)),
        ],
        out_specs=pl.BlockSpec((1, seq, hps * dv), lambda b, h: (b, 0, h)),
        out_shape=jax.ShapeDtypeStruct((batch, seq, hd * dv), BF16),
        scratch_shapes=[pltpu.VMEM((hps, dk, dv), F32)],
        compiler_params=_params("parallel", "parallel"),
        name="ret_core",
    )(qkvg3, qkvg3, qkvg3, qkvg3, dmat, qd, kd, gn)


def _mix_ffn_kernel(a_ref, wo_ref, res_ref, gain_ref, w_in_ref, cw_ref, cb_ref, w_out_ref, o_ref,
                    h_ref, act_ref, carry_ref, *, tiles_per_seq):
    i = pl.program_id(0)
    tm = res_ref.shape[0]
    ffn_dim = w_out_ref.shape[0]

    x1 = res_ref[...] + _dot(a_ref[...], wo_ref[...])
    o_ref[...] = x1
    h_ref[...] = (x1 * _rms_scale(x1) * gain_ref[...]).astype(BF16)

    seq_start = lax.rem(i, tiles_per_seq) == 0
    row = lax.broadcasted_iota(jnp.int32, (tm, 1), 0)
    for c in range(ffn_dim // FFN_CHUNK):
        lo = c * FFN_CHUNK
        a = _dot(h_ref[...], w_in_ref[:, lo:lo + FFN_CHUNK])
        g = _dot(h_ref[...], w_in_ref[:, ffn_dim + lo:ffn_dim + lo + FFN_CHUNK])
        prev = jnp.where(seq_start, 0.0, carry_ref[:, lo:lo + FFN_CHUNK])
        prev1 = prev[SUBLANES - 1:SUBLANES, :]
        prev2 = prev[SUBLANES - 2:SUBLANES - 1, :]
        g1 = jnp.where(row == 0, prev1, pltpu.roll(g, 1, 0))
        g2 = jnp.where(row == 0, prev2, jnp.where(row == 1, prev1, pltpu.roll(g, 2, 0)))
        carry_ref[:, lo:lo + FFN_CHUNK] = g[tm - SUBLANES:tm, :]
        cw = cw_ref[:, lo:lo + FFN_CHUNK]
        gc = g2 * cw[0:1, :] + g1 * cw[1:2, :] + g * cw[2:3, :] + cb_ref[:, lo:lo + FFN_CHUNK]
        act_ref[:, lo:lo + FFN_CHUNK] = (_silu(gc) * a).astype(BF16)

    o_ref[...] += _dot(act_ref[...], w_out_ref[...])


def _resident_layer(arr, layer):
    return pl.BlockSpec((None,) + arr.shape[1:], lambda i: (layer,) + (0,) * (arr.ndim - 1),
                        pipeline_mode=pl.Buffered(1))


def _mix_ffn(a, w_o, res, gain, w_in, conv_w, conv_b, w_out, seq, layer):
    t, d = res.shape
    ka = a.shape[1]
    f = w_out.shape[1]
    tm = FFN_TM
    kern = functools.partial(_mix_ffn_kernel, tiles_per_seq=seq // tm)
    return pl.pallas_call(
        kern,
        grid=(t // tm,),
        in_specs=[
            pl.BlockSpec((tm, ka), lambda i: (i, 0)),
            _resident_layer(w_o, 0),
            pl.BlockSpec((tm, d), lambda i: (i, 0)),
            _resident(gain), _resident_layer(w_in, layer), _resident(conv_w), _resident(conv_b),
            _resident_layer(w_out, layer),
        ],
        out_specs=pl.BlockSpec((tm, d), lambda i: (i, 0)),
        out_shape=jax.ShapeDtypeStruct((t, d), F32),
        scratch_shapes=[pltpu.VMEM((tm, d), BF16), pltpu.VMEM((tm, f), BF16), pltpu.VMEM((SUBLANES, f), F32)],
        compiler_params=_params("arbitrary"),
        name="mix_ffn",
    )(a, w_o, res, gain, w_in, conv_w, conv_b, w_out)


def _mla_proj_kernel(x_ref, gain_ref, w_in_ref, qn_ref, kvn_ref, wq_ref, wkv_ref,
                     c_ref, s_ref, gq_ref, gk_ref, q_ref, k_ref, v_ref):
    x = x_ref[...]
    h = (x * _rms_scale(x) * gain_ref[...]).astype(BF16)
    p = _dot(h, w_in_ref[...])
    c_q = p[:, :MLA_Q_RANK]
    c_kv = p[:, MLA_Q_RANK:MLA_Q_RANK + MLA_KV_RANK]
    k_r = p[:, MLA_Q_RANK + MLA_KV_RANK:MLA_Q_RANK + MLA_KV_RANK + LANES]
    k_s = p[:, MLA_Q_RANK + MLA_KV_RANK + LANES:]
    c_q = (c_q * _rms_scale(c_q) * qn_ref[...]).astype(BF16)
    c_kv = (c_kv * _rms_scale(c_kv) * kvn_ref[...]).astype(BF16)

    cos = c_ref[...]
    sin = s_ref[...]
    gq_nope, gq_rope, gq_swap = gq_ref[0:1, :], gq_ref[1:2, :], gq_ref[2:3, :]
    gk_nope, gk_rope, gk_swap = gk_ref[0:1, :], gk_ref[1:2, :], gk_ref[2:3, :]
    inv_dim = 1.0 / MLA_QK_DIM
    q_scale = MLA_QK_DIM ** -0.5 * LOG2_E

    k_rope_sq = k_r * k_r
    k_rope_rot = k_r * gk_rope * cos + k_s * gk_swap * sin

    def project(pair):
        q2 = _dot(c_q, wq_ref[:, pair * 2 * MLA_Q_COLS:(pair + 1) * 2 * MLA_Q_COLS])
        kv2 = _dot(c_kv, wkv_ref[:, pair * 2 * MLA_HEAD_PAD:(pair + 1) * 2 * MLA_HEAD_PAD])
        return q2, kv2

    def finish(pair, q2, kv2):
        for sub in range(2):
            hh = 2 * pair + sub
            q_n = q2[:, sub * MLA_Q_COLS:sub * MLA_Q_COLS + LANES]
            q_r = q2[:, sub * MLA_Q_COLS + LANES:sub * MLA_Q_COLS + 2 * LANES]
            q_s = q2[:, sub * MLA_Q_COLS + 2 * LANES:(sub + 1) * MLA_Q_COLS]
            ss = jnp.sum(q_n * q_n + q_r * q_r, axis=-1, keepdims=True)
            r = lax.rsqrt(ss * inv_dim + RMS_EPS) * q_scale
            q_ref[0, hh, :, 0:LANES] = (q_n * r * gq_nope).astype(BF16)
            q_ref[0, hh, :, LANES:2 * LANES] = ((q_r * gq_rope * cos + q_s * gq_swap * sin) * r).astype(BF16)

            k_n = kv2[:, sub * MLA_HEAD_PAD:sub * MLA_HEAD_PAD + LANES]
            ssk = jnp.sum(k_n * k_n + k_rope_sq, axis=-1, keepdims=True)
            rk = lax.rsqrt(ssk * inv_dim + RMS_EPS)
            k_ref[0, hh, :, 0:LANES] = (k_n * rk * gk_nope).astype(BF16)
            k_ref[0, hh, :, LANES:2 * LANES] = (k_rope_rot * rk).astype(BF16)
            v_ref[0, hh] = kv2[:, sub * MLA_HEAD_PAD + LANES:(sub + 1) * MLA_HEAD_PAD].astype(BF16)

    n_pairs = MLA_HEADS // 2
    pending = project(0)
    for pair in range(n_pairs):
        upcoming = project(pair + 1) if pair + 1 < n_pairs else None
        finish(pair, *pending)
        pending = upcoming


def _mla_proj(x2d, gain, w_in, qn, kvn, wq, wkv, cos, sin, gq, gk, batch, seq):
    t, d = x2d.shape
    tm = MLA_TM
    tps = seq // tm
    hd = MLA_HEADS

    return pl.pallas_call(
        _mla_proj_kernel,
        grid=(t // tm,),
        in_specs=[
            pl.BlockSpec((tm, d), lambda i: (i, 0)),
            _resident(gain), _resident(w_in), _resident(qn), _resident(kvn), _resident(wq), _resident(wkv),
            pl.BlockSpec((tm, LANES), lambda i: (i % tps, 0)),
            pl.BlockSpec((tm, LANES), lambda i: (i % tps, 0)),
            _resident(gq), _resident(gk),
        ],
        out_specs=[
            pl.BlockSpec((1, hd, tm, MLA_HEAD_PAD), lambda i: (i // tps, 0, i % tps, 0)),
            pl.BlockSpec((1, hd, tm, MLA_HEAD_PAD), lambda i: (i // tps, 0, i % tps, 0)),
            pl.BlockSpec((1, hd, tm, MLA_V_DIM), lambda i: (i // tps, 0, i % tps, 0)),
        ],
        out_shape=[
            jax.ShapeDtypeStruct((batch, hd, seq, MLA_HEAD_PAD), BF16),
            jax.ShapeDtypeStruct((batch, hd, seq, MLA_HEAD_PAD), BF16),
            jax.ShapeDtypeStruct((batch, hd, seq, MLA_V_DIM), BF16),
        ],
        compiler_params=_params("parallel"),
        name="mla_proj",
    )(x2d, gain, w_in, qn, kvn, wq, wkv, cos, sin, gq, gk)


def _attn_kernel(q_ref, k_ref, v_ref, o_ref, *, n_q_blocks):
    tq = ATT_TQ
    row = lax.broadcasted_iota(jnp.int32, (tq, tq), 0) // CHUNK
    col = lax.broadcasted_iota(jnp.int32, (tq, tq), 1) // CHUNK
    visible = col <= row

    def scores(hh, qi):
        q0 = qi * tq
        q = q_ref[0, hh, q0:q0 + tq, :]
        s_diag = jnp.where(visible, _dot_nt(q, k_ref[0, hh, q0:q0 + tq, :]), MASK_VALUE)
        s_off = _dot_nt(q, k_ref[0, hh, 0:q0, :]) if qi > 0 else None
        return s_diag, s_off

    def finish(hh, qi, s_diag, s_off):
        q0 = qi * tq
        m = jnp.max(s_diag, axis=-1, keepdims=True)
        if qi > 0:
            m = jnp.maximum(m, jnp.max(s_off, axis=-1, keepdims=True))
        p_diag = jnp.exp2(s_diag - m)
        l = jnp.sum(p_diag, axis=-1, keepdims=True)
        acc = _dot(p_diag.astype(BF16), v_ref[0, hh, q0:q0 + tq, :])
        if qi > 0:
            p_off = jnp.exp2(s_off - m)
            l = l + jnp.sum(p_off, axis=-1, keepdims=True)
            acc = acc + _dot(p_off.astype(BF16), v_ref[0, hh, 0:q0, :])
        o_ref[0, q0:q0 + tq, hh * MLA_V_DIM:(hh + 1) * MLA_V_DIM] = (acc / l).astype(BF16)

    work = [(hh, qi) for hh in range(ATT_HEADS_PER_STEP) for qi in range(n_q_blocks - 1, -1, -1)]
    queue = [scores(*item) for item in work[:ATT_AHEAD]]
    for pos, item in enumerate(work):
        if pos + ATT_AHEAD < len(work):
            queue.append(scores(*work[pos + ATT_AHEAD]))
        finish(*item, *queue.pop(0))


def _attention(q, k, v):
    batch, hd, seq, _ = q.shape
    hps = ATT_HEADS_PER_STEP
    kern = functools.partial(_attn_kernel, n_q_blocks=seq // ATT_TQ)
    return pl.pallas_call(
        kern,
        grid=(batch, hd // hps),
        in_specs=[
            pl.BlockSpec((1, hps, seq, MLA_HEAD_PAD), lambda b, h: (b, h, 0, 0)),
            pl.BlockSpec((1, hps, seq, MLA_HEAD_PAD), lambda b, h: (b, h, 0, 0)),
            pl.BlockSpec((1, hps, seq, MLA_V_DIM), lambda b, h: (b, h, 0, 0)),
        ],
        out_specs=pl.BlockSpec((1, seq, hps * MLA_V_DIM), lambda b, h: (b, 0, h)),
        out_shape=jax.ShapeDtypeStruct((batch, seq, hd * MLA_V_DIM), BF16),
        compiler_params=_params("parallel", "parallel"),
        name="mla_attention",
    )(q, k, v)


def _rope_tables(seq, half):
    inv_freq = ROPE_THETA ** (-jnp.arange(half, dtype=F32) / half)
    ang = jnp.arange(seq).astype(F32)[:, None] * inv_freq[None, :]
    return jnp.cos(ang), jnp.sin(ang)


def _retention_decay_tables():
    sc = RET_SUPER
    log_gamma = jnp.log1p(-jnp.exp2(RET_GAMMA_BASE - jnp.arange(RET_HEADS, dtype=F32)))
    idx = jnp.arange(sc, dtype=F32)
    dist = jnp.abs(idx[:, None] - idx[None, :])
    chunk = jnp.arange(sc) // CHUNK
    visible = chunk[None, :] <= chunk[:, None]
    dmat = jnp.where(visible[None], jnp.exp(log_gamma[:, None, None] * dist[None]), 0.0)
    qd = jnp.exp(log_gamma[:, None] * (idx + 1.0))[:, :, None]
    kd = jnp.exp(log_gamma[:, None] * (sc - 1.0 - idx))[:, :, None]
    return dmat, qd, kd


def _swap_halves(w):
    half = w.shape[-1] // 2
    return jnp.concatenate([w[..., half:], w[..., :half]], axis=-1)


def _pad_lanes(w):
    return jnp.pad(w, [(0, 0)] * (w.ndim - 1) + [(0, LANES - w.shape[-1])])


def _mla_weights(w_in, w_qb, q_head_g, k_head_g):
    rq, rkv = MLA_Q_RANK, MLA_KV_RANK
    w_kr = w_in[:, rq + rkv:]
    w_in_l = jnp.concatenate([w_in[:, :rq + rkv], _pad_lanes(w_kr), _pad_lanes(_swap_halves(w_kr))], axis=1)
    w_qb3 = w_qb.reshape(rq, MLA_HEADS, MLA_QK_DIM)
    nope, rope = w_qb3[..., :MLA_NOPE_DIM], w_qb3[..., MLA_NOPE_DIM:]
    wq = jnp.concatenate([nope, _pad_lanes(rope), _pad_lanes(_swap_halves(rope))], axis=-1)
    wq = wq.reshape(rq, MLA_HEADS * MLA_Q_COLS)

    def gains(g):
        rope_g = g[MLA_NOPE_DIM:]
        return jnp.stack([g[:MLA_NOPE_DIM], _pad_lanes(rope_g), _pad_lanes(_swap_halves(rope_g))])

    return w_in_l, wq, gains(q_head_g), gains(k_head_g)


def kernel(x, ret_norm, ret_w_in, ret_gn, ret_w_out, mla_norm, mla_w_in, mla_q_norm, mla_w_qb, mla_kv_norm,
           mla_w_kvb, mla_q_head_norm, mla_k_head_norm, mla_w_out, ffn_norm, ffn_w_in, ffn_conv_w, ffn_conv_b,
           ffn_w_out):
    batch, seq, d = x.shape
    x2d = x.reshape(batch * seq, d)

    cos_r, sin_r = _rope_tables(seq, RET_QK_DIM // 2)
    dmat, qd, kd = _retention_decay_tables()
    w_in_l, wq, gq, gk = _mla_weights(mla_w_in[0], mla_w_qb[0], mla_q_head_norm[0], mla_k_head_norm[0])
    side = [ret_w_out, ffn_w_in, ffn_w_out, w_in_l, wq, mla_w_kvb[0], mla_w_out]
    qkvg, (ret_w_out_b, ffn_w_in_b, ffn_w_out_b, w_in_l_b, wq_b, wkv_b, mla_w_out_b) = _ret_in_proj(
        x2d, ret_norm[0][None, :], ret_w_in, cos_r, sin_r, seq, side)
    ret = _ret_core(qkvg, dmat, qd, kd, ret_gn[0][:, None, :], batch, seq)
    x2d = _mix_ffn(ret.reshape(batch * seq, -1), ret_w_out_b, x2d, ffn_norm[0][None, :],
                   ffn_w_in_b, ffn_conv_w[0], ffn_conv_b[0][None, :], ffn_w_out_b, seq, layer=0)

    cos_m, sin_m = _rope_tables(seq, MLA_ROPE_DIM // 2)
    pad = jnp.zeros((seq, LANES - MLA_ROPE_DIM), F32)
    cos_t = jnp.concatenate([cos_m, cos_m, pad], axis=1)
    sin_t = jnp.concatenate([-sin_m, sin_m, pad], axis=1)
    q, k, v = _mla_proj(x2d, mla_norm[0][None, :], w_in_l_b, mla_q_norm[0][None, :],
                        mla_kv_norm[0][None, :], wq_b, wkv_b, cos_t, sin_t, gq, gk, batch, seq)
    att = _attention(q, k, v)
    x2d = _mix_ffn(att.reshape(batch * seq, -1), mla_w_out_b, x2d, ffn_norm[1][None, :],
                   ffn_w_in_b, ffn_conv_w[1], ffn_conv_b[1][None, :], ffn_w_out_b, seq, layer=1)
    return x2d.reshape(batch, seq, d)
```

```python
import functools

import jax
import jax.numpy as jnp
from jax import lax
from jax.experimental import pallas as pl
from jax.experimental.pallas import tpu as pltpu

F32 = jnp.float32
BF16 = jnp.bfloat16

CHUNK = 64
RMS_EPS = 1e-6
ROPE_THETA = 10000.0
RET_HEADS = 4
RET_QK_DIM = 256
RET_V_DIM = 512
RET_GAMMA_BASE = -5.0
MLA_HEADS = 8
MLA_Q_RANK = 384
MLA_KV_RANK = 256
MLA_NOPE_DIM = 128
MLA_ROPE_DIM = 64
MLA_V_DIM = 128
MLA_QK_DIM = MLA_NOPE_DIM + MLA_ROPE_DIM
MLA_HEAD_PAD = 256
MLA_Q_COLS = 3 * 128
MASK_VALUE = -1e30
CONV_WIDTH = 3
LOG2_E = 1.4426950408889634

LANES = 128
SUBLANES = 8
VMEM_LIMIT_BYTES = 52 * 1024 * 1024
BF16_ROWS = 16

RET_SUPER = 256
PROJ_TM = 512
RET_HEADS_PER_STEP = 2
FFN_TM = 512
FFN_CHUNK = 256
MLA_TM = 512
W_STAGE_BYTES = 384 * 1024
W_STAGE_SLOTS = 4
ATT_TQ = 256
ATT_HEADS_PER_STEP = 2
ATT_AHEAD = 2


def _params(*sem):
    return pltpu.CompilerParams(dimension_semantics=sem, vmem_limit_bytes=VMEM_LIMIT_BYTES)


def _resident(arr):
    return pl.BlockSpec(arr.shape, lambda i: (0,) * arr.ndim, pipeline_mode=pl.Buffered(1))


def _stage_rows(rows, cols):
    limit = W_STAGE_BYTES // (cols * 4)
    fits = [r for r in range(BF16_ROWS, limit + 1, BF16_ROWS) if rows % r == 0]
    return max(fits)


def _weight_scratch(rows, cols):
    return [pltpu.VMEM((rows, cols), BF16), pltpu.VMEM((W_STAGE_SLOTS, _stage_rows(rows, cols), cols), F32),
            pltpu.SemaphoreType.DMA((W_STAGE_SLOTS,))]


def _load_weight(src_ref, dst_ref, stage_ref, sem_ref):
    slots, rows = stage_ref.shape[0], stage_ref.shape[1]
    n_chunks = dst_ref.shape[0] // rows
    lookahead = slots - 1
    assert n_chunks >= lookahead

    def copy(k, slot):
        return pltpu.make_async_copy(src_ref.at[pl.ds(k * rows, rows), :], stage_ref.at[slot], sem_ref.at[slot])

    for k in range(lookahead):
        copy(k, k).start()

    def body(k, carry):
        nxt = k + lookahead

        @pl.when(nxt < n_chunks)
        def _():
            copy(nxt, lax.rem(nxt, slots)).start()

        slot = lax.rem(k, slots)
        copy(k, slot).wait()
        r0 = pl.multiple_of(k * rows, rows)
        dst_ref[pl.ds(r0, rows), :] = stage_ref[slot].astype(BF16)
        return carry

    lax.fori_loop(0, n_chunks, body, 0)


def _rms_scale(x):
    return lax.rsqrt(jnp.mean(x * x, axis=-1, keepdims=True) + RMS_EPS)


def _silu(x):
    half = 0.5 * x
    return half + half * jnp.tanh(half)


def _dot(a, b):
    return jnp.dot(a, b, preferred_element_type=F32)


def _dot_nt(a, b):
    return lax.dot_general(a, b, (((1,), (1,)), ((), ())), preferred_element_type=F32)


def _dot_tn(a, b):
    return lax.dot_general(a, b, (((0,), (0,)), ((), ())), preferred_element_type=F32)


def _ret_in_kernel(*refs, n_side):
    x_ref, gain_ref, w_hbm, cos_ref, sin_ref = refs[:5]
    side_in = refs[5:5 + n_side]
    o_ref = refs[5 + n_side]
    side_out = refs[6 + n_side:6 + 2 * n_side]
    h_ref, w_ref, stage_ref, sem_ref = refs[6 + 2 * n_side:]
    i = pl.program_id(0)
    half = RET_QK_DIM // 2

    @pl.when(i == 0)
    def _():
        _load_weight(w_hbm.at[0], w_ref, stage_ref, sem_ref)

    for src_ref, dst_ref in zip(side_in, side_out):
        dst_ref[...] = src_ref[...].astype(BF16)

    x = x_ref[...]
    h_ref[...] = (x * _rms_scale(x) * gain_ref[...]).astype(BF16)
    cos = cos_ref[...]
    sin = sin_ref[...]
    k_scale = RET_QK_DIM ** -0.5
    cos_k = cos * k_scale
    sin_k = sin * k_scale
    for c in range(w_ref.shape[1] // RET_QK_DIM):
        lo = c * RET_QK_DIM
        acc = _dot(h_ref[...], w_ref[:, lo:lo + RET_QK_DIM])
        if c < 2 * RET_HEADS:
            cs, sn = (cos, sin) if c < RET_HEADS else (cos_k, sin_k)
            x1 = acc[:, :half]
            x2 = acc[:, half:]
            o_ref[:, lo:lo + half] = (x1 * cs - x2 * sn).astype(BF16)
            o_ref[:, lo + half:lo + RET_QK_DIM] = (x2 * cs + x1 * sn).astype(BF16)
        else:
            o_ref[:, lo:lo + RET_QK_DIM] = acc.astype(BF16)


def _cast_plan(arr, n_grid):
    cols = arr.shape[-1]
    rows = arr.size // cols
    steps = max(s for s in range(1, n_grid + 1) if rows % s == 0 and (rows // s) % BF16_ROWS == 0)
    return arr.reshape(rows, cols), steps, rows // steps


def _ret_in_proj(x2d, gain, w, cos, sin, seq, side_weights):
    t, d = x2d.shape
    n = w.shape[-1]
    tm = PROJ_TM
    tps = seq // tm
    n_grid = t // tm
    plans = [_cast_plan(arr, n_grid) for arr in side_weights]
    side_specs = [pl.BlockSpec((rps, view.shape[1]), lambda i, last=steps - 1: (jnp.minimum(i, last), 0))
                  for view, steps, rps in plans]
    kern = functools.partial(_ret_in_kernel, n_side=len(plans))
    outs = pl.pallas_call(
        kern,
        grid=(n_grid,),
        in_specs=[
            pl.BlockSpec((tm, d), lambda i: (i, 0)),
            _resident(gain), pl.BlockSpec(memory_space=pl.ANY),
            pl.BlockSpec((tm, RET_QK_DIM // 2), lambda i: (i % tps, 0)),
            pl.BlockSpec((tm, RET_QK_DIM // 2), lambda i: (i % tps, 0)),
        ] + side_specs,
        out_specs=[pl.BlockSpec((tm, n), lambda i: (i, 0))] + side_specs,
        out_shape=[jax.ShapeDtypeStruct((t, n), BF16)]
        + [jax.ShapeDtypeStruct(view.shape, BF16) for view, _, _ in plans],
        scratch_shapes=[pltpu.VMEM((tm, d), BF16)] + _weight_scratch(d, n),
        compiler_params=_params("arbitrary"),
        name="ret_in_proj",
    )(x2d, gain, w, cos, sin, *[view for view, _, _ in plans])
    return outs[0], [o.reshape(arr.shape) for o, arr in zip(outs[1:], side_weights)]


def _ret_core_kernel(q_ref, k_ref, v_ref, g_ref, dmat_ref, qd_ref, kd_ref, gn_ref, o_ref, state_ref, *, n_steps):
    dk, dv = RET_QK_DIM, RET_V_DIM
    state_ref[...] = jnp.zeros_like(state_ref)

    for sc in range(n_steps):
        rows = slice(sc * RET_SUPER, (sc + 1) * RET_SUPER)
        for hh in range(RET_HEADS_PER_STEP):
            qd = qd_ref[hh]
            kd = kd_ref[hh]
            step_decay = qd[RET_SUPER - 1:RET_SUPER, :]
            q = q_ref[0, rows, hh * dk:(hh + 1) * dk]
            k = k_ref[0, rows, hh * dk:(hh + 1) * dk]
            v = v_ref[0, rows, hh * dv:(hh + 1) * dv]
            scores = _dot_nt(q, k) * dmat_ref[hh]
            inner = _dot(scores.astype(BF16), v)
            state = state_ref[hh]
            q_scaled = (q.astype(F32) * qd).astype(BF16)
            cross = _dot(q_scaled, state.astype(BF16))
            k_scaled = (k.astype(F32) * kd).astype(BF16)
            state_ref[hh] = state * step_decay + _dot_tn(k_scaled, v)
            out = inner + cross
            out = out * _rms_scale(out) * gn_ref[hh]
            g = g_ref[0, rows, hh * dv:(hh + 1) * dv]
            o_ref[0, rows, hh * dv:(hh + 1) * dv] = out.astype(BF16) * _silu(g)


def _ret_core(qkvg, dmat, qd, kd, gn, batch, seq):
    hd, dk, dv = RET_HEADS, RET_QK_DIM, RET_V_DIM
    hps = RET_HEADS_PER_STEP
    groups = hd // hps
    qkvg3 = qkvg.reshape(batch, seq, qkvg.shape[-1])
    k_blk0 = groups
    v_blk0 = 2 * hd * dk // (hps * dv)
    g_blk0 = v_blk0 + groups
    kern = functools.partial(_ret_core_kernel, n_steps=seq // RET_SUPER)
    return pl.pallas_call(
        kern,
        grid=(batch, groups),
        in_specs=[
            pl.BlockSpec((1, seq, hps * dk), lambda b, h: (b, 0, h)),
            pl.BlockSpec((1, seq, hps * dk), lambda b, h: (b, 0, k_blk0 + h)),
            pl.BlockSpec((1, seq, hps * dv), lambda b, h: (b, 0, v_blk0 + h)),
            pl.BlockSpec((1, seq, hps * dv), lambda b, h: (b, 0, g_blk0 + h)),
            pl.BlockSpec((hps, RET_SUPER, RET_SUPER), lambda b, h: (h, 0, 0)),
            pl.BlockSpec((hps, RET_SUPER, 1), lambda b, h: (h, 0, 0)),
            pl.BlockSpec((hps, RET_SUPER, 1), lambda b, h: (h, 0, 0)),
            pl.BlockSpec((hps, 1, dv), lambda b, h: (h, 0, 0)),
        ],
        out_specs=pl.BlockSpec((1, seq, hps * dv), lambda b, h: (b, 0, h)),
        out_shape=jax.ShapeDtypeStruct((batch, seq, hd * dv), BF16),
        scratch_shapes=[pltpu.VMEM((hps, dk, dv), F32)],
        compiler_params=_params("parallel", "parallel"),
        name="ret_core",
    )(qkvg3, qkvg3, qkvg3, qkvg3, dmat, qd, kd, gn)


def _mix_ffn_kernel(a_ref, wo_ref, res_ref, gain_ref, w_in_ref, cw_ref, cb_ref, w_out_ref, o_ref,
                    h_ref, act_ref, carry_ref, *, tiles_per_seq):
    i = pl.program_id(0)
    tm = res_ref.shape[0]
    ffn_dim = w_out_ref.shape[0]

    x1 = res_ref[...] + _dot(a_ref[...], wo_ref[...])
    o_ref[...] = x1
    h_ref[...] = (x1 * _rms_scale(x1) * gain_ref[...]).astype(BF16)

    seq_start = lax.rem(i, tiles_per_seq) == 0
    row = lax.broadcasted_iota(jnp.int32, (tm, 1), 0)
    for c in range(ffn_dim // FFN_CHUNK):
        lo = c * FFN_CHUNK
        a = _dot(h_ref[...], w_in_ref[:, lo:lo + FFN_CHUNK])
        g = _dot(h_ref[...], w_in_ref[:, ffn_dim + lo:ffn_dim + lo + FFN_CHUNK])
        prev = jnp.where(seq_start, 0.0, carry_ref[:, lo:lo + FFN_CHUNK])
        prev1 = prev[SUBLANES - 1:SUBLANES, :]
        prev2 = prev[SUBLANES - 2:SUBLANES - 1, :]
        g1 = jnp.where(row == 0, prev1, pltpu.roll(g, 1, 0))
        g2 = jnp.where(row == 0, prev2, jnp.where(row == 1, prev1, pltpu.roll(g, 2, 0)))
        carry_ref[:, lo:lo + FFN_CHUNK] = g[tm - SUBLANES:tm, :]
        cw = cw_ref[:, lo:lo + FFN_CHUNK]
        gc = g2 * cw[0:1, :] + g1 * cw[1:2, :] + g * cw[2:3, :] + cb_ref[:, lo:lo + FFN_CHUNK]
        act_ref[:, lo:lo + FFN_CHUNK] = (_silu(gc) * a).astype(BF16)

    o_ref[...] += _dot(act_ref[...], w_out_ref[...])


def _resident_layer(arr, layer):
    return pl.BlockSpec((None,) + arr.shape[1:], lambda i: (layer,) + (0,) * (arr.ndim - 1),
                        pipeline_mode=pl.Buffered(1))


def _mix_ffn(a, w_o, res, gain, w_in, conv_w, conv_b, w_out, seq, layer):
    t, d = res.shape
    ka = a.shape[1]
    f = w_out.shape[1]
    tm = FFN_TM
    kern = functools.partial(_mix_ffn_kernel, tiles_per_seq=seq // tm)
    return pl.pallas_call(
        kern,
        grid=(t // tm,),
        in_specs=[
            pl.BlockSpec((tm, ka), lambda i: (i, 0)),
            _resident_layer(w_o, 0),
            pl.BlockSpec((tm, d), lambda i: (i, 0)),
            _resident(gain), _resident_layer(w_in, layer), _resident(conv_w), _resident(conv_b),
            _resident_layer(w_out, layer),
        ],
        out_specs=pl.BlockSpec((tm, d), lambda i: (i, 0)),
        out_shape=jax.ShapeDtypeStruct((t, d), F32),
        scratch_shapes=[pltpu.VMEM((tm, d), BF16), pltpu.VMEM((tm, f), BF16), pltpu.VMEM((SUBLANES, f), F32)],
        compiler_params=_params("arbitrary"),
        name="mix_ffn",
    )(a, w_o, res, gain, w_in, conv_w, conv_b, w_out)


def _mla_proj_kernel(x_ref, gain_ref, w_in_ref, qn_ref, kvn_ref, wq_ref, wkv_ref,
                     c_ref, s_ref, gq_ref, gk_ref, q_ref, k_ref, v_ref):
    x = x_ref[...]
    h = (x * _rms_scale(x) * gain_ref[...]).astype(BF16)
    p = _dot(h, w_in_ref[...])
    c_q = p[:, :MLA_Q_RANK]
    c_kv = p[:, MLA_Q_RANK:MLA_Q_RANK + MLA_KV_RANK]
    k_r = p[:, MLA_Q_RANK + MLA_KV_RANK:MLA_Q_RANK + MLA_KV_RANK + LANES]
    k_s = p[:, MLA_Q_RANK + MLA_KV_RANK + LANES:]
    c_q = (c_q * _rms_scale(c_q) * qn_ref[...]).astype(BF16)
    c_kv = (c_kv * _rms_scale(c_kv) * kvn_ref[...]).astype(BF16)

    cos = c_ref[...]
    sin = s_ref[...]
    gq_nope, gq_rope, gq_swap = gq_ref[0:1, :], gq_ref[1:2, :], gq_ref[2:3, :]
    gk_nope, gk_rope, gk_swap = gk_ref[0:1, :], gk_ref[1:2, :], gk_ref[2:3, :]
    inv_dim = 1.0 / MLA_QK_DIM
    q_scale = MLA_QK_DIM ** -0.5 * LOG2_E

    k_rope_sq = k_r * k_r
    k_rope_rot = k_r * gk_rope * cos + k_s * gk_swap * sin

    def project(pair):
        q2 = _dot(c_q, wq_ref[:, pair * 2 * MLA_Q_COLS:(pair + 1) * 2 * MLA_Q_COLS])
        kv2 = _dot(c_kv, wkv_ref[:, pair * 2 * MLA_HEAD_PAD:(pair + 1) * 2 * MLA_HEAD_PAD])
        return q2, kv2

    def finish(pair, q2, kv2):
        for sub in range(2):
            hh = 2 * pair + sub
            q_n = q2[:, sub * MLA_Q_COLS:sub * MLA_Q_COLS + LANES]
            q_r = q2[:, sub * MLA_Q_COLS + LANES:sub * MLA_Q_COLS + 2 * LANES]
            q_s = q2[:, sub * MLA_Q_COLS + 2 * LANES:(sub + 1) * MLA_Q_COLS]
            ss = jnp.sum(q_n * q_n + q_r * q_r, axis=-1, keepdims=True)
            r = lax.rsqrt(ss * inv_dim + RMS_EPS) * q_scale
            q_ref[0, hh, :, 0:LANES] = (q_n * r * gq_nope).astype(BF16)
            q_ref[0, hh, :, LANES:2 * LANES] = ((q_r * gq_rope * cos + q_s * gq_swap * sin) * r).astype(BF16)

            k_n = kv2[:, sub * MLA_HEAD_PAD:sub * MLA_HEAD_PAD + LANES]
            ssk = jnp.sum(k_n * k_n + k_rope_sq, axis=-1, keepdims=True)
            rk = lax.rsqrt(ssk * inv_dim + RMS_EPS)
            k_ref[0, hh, :, 0:LANES] = (k_n * rk * gk_nope).astype(BF16)
            k_ref[0, hh, :, LANES:2 * LANES] = (k_rope_rot * rk).astype(BF16)
            v_ref[0, hh] = kv2[:, sub * MLA_HEAD_PAD + LANES:(sub + 1) * MLA_HEAD_PAD].astype(BF16)

    n_pairs = MLA_HEADS // 2
    pending = project(0)
    for pair in range(n_pairs):
        upcoming = project(pair + 1) if pair + 1 < n_pairs else None
        finish(pair, *pending)
        pending = upcoming


def _mla_proj(x2d, gain, w_in, qn, kvn, wq, wkv, cos, sin, gq, gk, batch, seq):
    t, d = x2d.shape
    tm = MLA_TM
    tps = seq // tm
    hd = MLA_HEADS

    return pl.pallas_call(
        _mla_proj_kernel,
        grid=(t // tm,),
        in_specs=[
            pl.BlockSpec((tm, d), lambda i: (i, 0)),
            _resident(gain), _resident(w_in), _resident(qn), _resident(kvn), _resident(wq), _resident(wkv),
            pl.BlockSpec((tm, LANES), lambda i: (i % tps, 0)),
            pl.BlockSpec((tm, LANES), lambda i: (i % tps, 0)),
            _resident(gq), _resident(gk),
        ],
        out_specs=[
            pl.BlockSpec((1, hd, tm, MLA_HEAD_PAD), lambda i: (i // tps, 0, i % tps, 0)),
            pl.BlockSpec((1, hd, tm, MLA_HEAD_PAD), lambda i: (i // tps, 0, i % tps, 0)),
            pl.BlockSpec((1, hd, tm, MLA_V_DIM), lambda i: (i // tps, 0, i % tps, 0)),
        ],
        out_shape=[
            jax.ShapeDtypeStruct((batch, hd, seq, MLA_HEAD_PAD), BF16),
            jax.ShapeDtypeStruct((batch, hd, seq, MLA_HEAD_PAD), BF16),
            jax.ShapeDtypeStruct((batch, hd, seq, MLA_V_DIM), BF16),
        ],
        compiler_params=_params("parallel"),
        name="mla_proj",
    )(x2d, gain, w_in, qn, kvn, wq, wkv, cos, sin, gq, gk)


def _attn_kernel(q_ref, k_ref, v_ref, o_ref, vext_ref, *, n_q_blocks):
    tq = ATT_TQ
    row = lax.broadcasted_iota(jnp.int32, (tq, tq), 0) // CHUNK
    col = lax.broadcasted_iota(jnp.int32, (tq, tq), 1) // CHUNK
    visible = col <= row

    vext_ref[:, :, :MLA_V_DIM] = v_ref[0]
    vext_ref[:, :, MLA_V_DIM:] = jnp.ones(vext_ref.shape[:2] + (MLA_V_DIM,), BF16)

    def scores(hh, qi):
        q0 = qi * tq
        q = q_ref[0, hh, q0:q0 + tq, :]
        s_diag = jnp.where(visible, _dot_nt(q, k_ref[0, hh, q0:q0 + tq, :]), MASK_VALUE)
        s_off = _dot_nt(q, k_ref[0, hh, 0:q0, :]) if qi > 0 else None
        return s_diag, s_off

    def finish(hh, qi, s_diag, s_off):
        q0 = qi * tq
        m = jnp.max(s_diag, axis=-1, keepdims=True)
        if qi > 0:
            m = jnp.maximum(m, jnp.max(s_off, axis=-1, keepdims=True))
        p_diag = jnp.exp2(s_diag - m)
        acc = _dot(p_diag.astype(BF16), vext_ref[hh, q0:q0 + tq, :])
        if qi > 0:
            p_off = jnp.exp2(s_off - m)
            acc = acc + _dot(p_off.astype(BF16), vext_ref[hh, 0:q0, :])
        out = acc[:, :MLA_V_DIM] / acc[:, MLA_V_DIM:]
        o_ref[0, q0:q0 + tq, hh * MLA_V_DIM:(hh + 1) * MLA_V_DIM] = out.astype(BF16)

    work = [(hh, qi) for hh in range(ATT_HEADS_PER_STEP) for qi in range(n_q_blocks - 1, -1, -1)]
    queue = [scores(*item) for item in work[:ATT_AHEAD]]
    for pos, item in enumerate(work):
        if pos + ATT_AHEAD < len(work):
            queue.append(scores(*work[pos + ATT_AHEAD]))
        finish(*item, *queue.pop(0))


def _attention(q, k, v):
    batch, hd, seq, _ = q.shape
    hps = ATT_HEADS_PER_STEP
    kern = functools.partial(_attn_kernel, n_q_blocks=seq // ATT_TQ)
    return pl.pallas_call(
        kern,
        grid=(batch, hd // hps),
        in_specs=[
            pl.BlockSpec((1, hps, seq, MLA_HEAD_PAD), lambda b, h: (b, h, 0, 0)),
            pl.BlockSpec((1, hps, seq, MLA_HEAD_PAD), lambda b, h: (b, h, 0, 0)),
            pl.BlockSpec((1, hps, seq, MLA_V_DIM), lambda b, h: (b, h, 0, 0)),
        ],
        out_specs=pl.BlockSpec((1, seq, hps * MLA_V_DIM), lambda b, h: (b, 0, h)),
        out_shape=jax.ShapeDtypeStruct((batch, seq, hd * MLA_V_DIM), BF16),
        scratch_shapes=[pltpu.VMEM((hps, seq, 2 * MLA_V_DIM), BF16)],
        compiler_params=_params("parallel", "parallel"),
        name="mla_attention",
    )(q, k, v)


def _rope_tables(seq, half):
    inv_freq = ROPE_THETA ** (-jnp.arange(half, dtype=F32) / half)
    ang = jnp.arange(seq).astype(F32)[:, None] * inv_freq[None, :]
    return jnp.cos(ang), jnp.sin(ang)


def _retention_decay_tables():
    sc = RET_SUPER
    log_gamma = jnp.log1p(-jnp.exp2(RET_GAMMA_BASE - jnp.arange(RET_HEADS, dtype=F32)))
    idx = jnp.arange(sc, dtype=F32)
    dist = jnp.abs(idx[:, None] - idx[None, :])
    chunk = jnp.arange(sc) // CHUNK
    visible = chunk[None, :] <= chunk[:, None]
    dmat = jnp.where(visible[None], jnp.exp(log_gamma[:, None, None] * dist[None]), 0.0)
    qd = jnp.exp(log_gamma[:, None] * (idx + 1.0))[:, :, None]
    kd = jnp.exp(log_gamma[:, None] * (sc - 1.0 - idx))[:, :, None]
    return dmat, qd, kd


def _swap_halves(w):
    half = w.shape[-1] // 2
    return jnp.concatenate([w[..., half:], w[..., :half]], axis=-1)


def _pad_lanes(w):
    return jnp.pad(w, [(0, 0)] * (w.ndim - 1) + [(0, LANES - w.shape[-1])])


def _mla_weights(w_in, w_qb, q_head_g, k_head_g):
    rq, rkv = MLA_Q_RANK, MLA_KV_RANK
    w_kr = w_in[:, rq + rkv:]
    w_in_l = jnp.concatenate([w_in[:, :rq + rkv], _pad_lanes(w_kr), _pad_lanes(_swap_halves(w_kr))], axis=1)
    w_qb3 = w_qb.reshape(rq, MLA_HEADS, MLA_QK_DIM)
    nope, rope = w_qb3[..., :MLA_NOPE_DIM], w_qb3[..., MLA_NOPE_DIM:]
    wq = jnp.concatenate([nope, _pad_lanes(rope), _pad_lanes(_swap_halves(rope))], axis=-1)
    wq = wq.reshape(rq, MLA_HEADS * MLA_Q_COLS)

    def gains(g):
        rope_g = g[MLA_NOPE_DIM:]
        return jnp.stack([g[:MLA_NOPE_DIM], _pad_lanes(rope_g), _pad_lanes(_swap_halves(rope_g))])

    return w_in_l, wq, gains(q_head_g), gains(k_head_g)


def kernel(x, ret_norm, ret_w_in, ret_gn, ret_w_out, mla_norm, mla_w_in, mla_q_norm, mla_w_qb, mla_kv_norm,
           mla_w_kvb, mla_q_head_norm, mla_k_head_norm, mla_w_out, ffn_norm, ffn_w_in, ffn_conv_w, ffn_conv_b,
           ffn_w_out):
    batch, seq, d = x.shape
    x2d = x.reshape(batch * seq, d)

    cos_r, sin_r = _rope_tables(seq, RET_QK_DIM // 2)
    dmat, qd, kd = _retention_decay_tables()
    w_in_l, wq, gq, gk = _mla_weights(mla_w_in[0], mla_w_qb[0], mla_q_head_norm[0], mla_k_head_norm[0])
    side = [ret_w_out, ffn_w_in, ffn_w_out, w_in_l, wq, mla_w_kvb[0], mla_w_out]
    qkvg, (ret_w_out_b, ffn_w_in_b, ffn_w_out_b, w_in_l_b, wq_b, wkv_b, mla_w_out_b) = _ret_in_proj(
        x2d, ret_norm[0][None, :], ret_w_in, cos_r, sin_r, seq, side)
    ret = _ret_core(qkvg, dmat, qd, kd, ret_gn[0][:, None, :], batch, seq)
    x2d = _mix_ffn(ret.reshape(batch * seq, -1), ret_w_out_b, x2d, ffn_norm[0][None, :],
                   ffn_w_in_b, ffn_conv_w[0], ffn_conv_b[0][None, :], ffn_w_out_b, seq, layer=0)

    cos_m, sin_m = _rope_tables(seq, MLA_ROPE_DIM // 2)
    pad = jnp.zeros((seq, LANES - MLA_ROPE_DIM), F32)
    cos_t = jnp.concatenate([cos_m, cos_m, pad], axis=1)
    sin_t = jnp.concatenate([-sin_m, sin_m, pad], axis=1)
    q, k, v = _mla_proj(x2d, mla_norm[0][None, :], w_in_l_b, mla_q_norm[0][None, :],
                        mla_kv_norm[0][None, :], wq_b, wkv_b, cos_t, sin_t, gq, gk, batch, seq)
    att = _attention(q, k, v)
    x2d = _mix_ffn(att.reshape(batch * seq, -1), mla_w_out_b, x2d, ffn_norm[1][None, :],
                   ffn_w_in_b, ffn_conv_w[1], ffn_conv_b[1][None, :], ffn_w_out_b, seq, layer=1)
    return x2d.reshape(batch, seq, d)
```

```python
import functools

import jax
import jax.numpy as jnp
from jax import lax
from jax.experimental import pallas as pl
from jax.experimental.pallas import tpu as pltpu

F32 = jnp.float32
BF16 = jnp.bfloat16

CHUNK = 64
RMS_EPS = 1e-6
ROPE_THETA = 10000.0
RET_HEADS = 4
RET_QK_DIM = 256
RET_V_DIM = 512
RET_GAMMA_BASE = -5.0
MLA_HEADS = 8
MLA_Q_RANK = 384
MLA_KV_RANK = 256
MLA_NOPE_DIM = 128
MLA_ROPE_DIM = 64
MLA_V_DIM = 128
MLA_QK_DIM = MLA_NOPE_DIM + MLA_ROPE_DIM
MLA_HEAD_PAD = 256
MLA_Q_COLS = 3 * 128
MASK_VALUE = -1e30
CONV_WIDTH = 3
LOG2_E = 1.4426950408889634

LANES = 128
SUBLANES = 8
VMEM_LIMIT_BYTES = 52 * 1024 * 1024
BF16_ROWS = 16

RET_SUPER = 256
PROJ_TM = 512
RET_HEADS_PER_STEP = 2
FFN_TM = 512
FFN_CHUNK = 256
MLA_TM = 512
W_STAGE_BYTES = 384 * 1024
W_STAGE_SLOTS = 4
ATT_TQ = 256
ATT_HEADS_PER_STEP = 2
ATT_AHEAD = 2


def _params(*sem):
    return pltpu.CompilerParams(dimension_semantics=sem, vmem_limit_bytes=VMEM_LIMIT_BYTES)


def _resident(arr):
    return pl.BlockSpec(arr.shape, lambda i: (0,) * arr.ndim, pipeline_mode=pl.Buffered(1))


def _stage_rows(rows, cols):
    limit = W_STAGE_BYTES // (cols * 4)
    fits = [r for r in range(BF16_ROWS, limit + 1, BF16_ROWS) if rows % r == 0]
    return max(fits)


def _weight_scratch(rows, cols):
    return [pltpu.VMEM((rows, cols), BF16), pltpu.VMEM((W_STAGE_SLOTS, _stage_rows(rows, cols), cols), F32),
            pltpu.SemaphoreType.DMA((W_STAGE_SLOTS,))]


def _load_weight(src_ref, dst_ref, stage_ref, sem_ref):
    slots, rows = stage_ref.shape[0], stage_ref.shape[1]
    n_chunks = dst_ref.shape[0] // rows
    lookahead = slots - 1
    assert n_chunks >= lookahead

    def copy(k, slot):
        return pltpu.make_async_copy(src_ref.at[pl.ds(k * rows, rows), :], stage_ref.at[slot], sem_ref.at[slot])

    for k in range(lookahead):
        copy(k, k).start()

    def body(k, carry):
        nxt = k + lookahead

        @pl.when(nxt < n_chunks)
        def _():
            copy(nxt, lax.rem(nxt, slots)).start()

        slot = lax.rem(k, slots)
        copy(k, slot).wait()
        r0 = pl.multiple_of(k * rows, rows)
        dst_ref[pl.ds(r0, rows), :] = stage_ref[slot].astype(BF16)
        return carry

    lax.fori_loop(0, n_chunks, body, 0)


def _rms_scale(x):
    return lax.rsqrt(jnp.mean(x * x, axis=-1, keepdims=True) + RMS_EPS)


def _silu(x):
    half = 0.5 * x
    return half + half * jnp.tanh(half)


def _dot(a, b):
    return jnp.dot(a, b, preferred_element_type=F32)


def _dot_nt(a, b):
    return lax.dot_general(a, b, (((1,), (1,)), ((), ())), preferred_element_type=F32)


def _dot_tn(a, b):
    return lax.dot_general(a, b, (((0,), (0,)), ((), ())), preferred_element_type=F32)


def _ret_in_kernel(*refs, n_side):
    x_ref, gain_ref, w_hbm, cos_ref, sin_ref = refs[:5]
    side_in = refs[5:5 + n_side]
    o_ref = refs[5 + n_side]
    side_out = refs[6 + n_side:6 + 2 * n_side]
    h_ref, w_ref, stage_ref, sem_ref = refs[6 + 2 * n_side:]
    i = pl.program_id(0)
    half = RET_QK_DIM // 2

    @pl.when(i == 0)
    def _():
        _load_weight(w_hbm.at[0], w_ref, stage_ref, sem_ref)

    for src_ref, dst_ref in zip(side_in, side_out):
        dst_ref[...] = src_ref[...].astype(BF16)

    x = x_ref[...]
    h_ref[...] = (x * _rms_scale(x) * gain_ref[...]).astype(BF16)
    cos = cos_ref[...]
    sin = sin_ref[...]
    k_scale = RET_QK_DIM ** -0.5
    cos_k = cos * k_scale
    sin_k = sin * k_scale
    for c in range(w_ref.shape[1] // RET_QK_DIM):
        lo = c * RET_QK_DIM
        acc = _dot(h_ref[...], w_ref[:, lo:lo + RET_QK_DIM])
        if c < 2 * RET_HEADS:
            cs, sn = (cos, sin) if c < RET_HEADS else (cos_k, sin_k)
            x1 = acc[:, :half]
            x2 = acc[:, half:]
            o_ref[:, lo:lo + half] = (x1 * cs - x2 * sn).astype(BF16)
            o_ref[:, lo + half:lo + RET_QK_DIM] = (x2 * cs + x1 * sn).astype(BF16)
        else:
            o_ref[:, lo:lo + RET_QK_DIM] = acc.astype(BF16)


def _cast_plan(arr, n_grid):
    cols = arr.shape[-1]
    rows = arr.size // cols
    steps = max(s for s in range(1, n_grid + 1) if rows % s == 0 and (rows // s) % BF16_ROWS == 0)
    return arr.reshape(rows, cols), steps, rows // steps


def _ret_in_proj(x2d, gain, w, cos, sin, seq, side_weights):
    t, d = x2d.shape
    n = w.shape[-1]
    tm = PROJ_TM
    tps = seq // tm
    n_grid = t // tm
    plans = [_cast_plan(arr, n_grid) for arr in side_weights]
    side_specs = [pl.BlockSpec((rps, view.shape[1]), lambda i, last=steps - 1: (jnp.minimum(i, last), 0))
                  for view, steps, rps in plans]
    kern = functools.partial(_ret_in_kernel, n_side=len(plans))
    outs = pl.pallas_call(
        kern,
        grid=(n_grid,),
        in_specs=[
            pl.BlockSpec((tm, d), lambda i: (i, 0)),
            _resident(gain), pl.BlockSpec(memory_space=pl.ANY),
            pl.BlockSpec((tm, RET_QK_DIM // 2), lambda i: (i % tps, 0)),
            pl.BlockSpec((tm, RET_QK_DIM // 2), lambda i: (i % tps, 0)),
        ] + side_specs,
        out_specs=[pl.BlockSpec((tm, n), lambda i: (i, 0))] + side_specs,
        out_shape=[jax.ShapeDtypeStruct((t, n), BF16)]
        + [jax.ShapeDtypeStruct(view.shape, BF16) for view, _, _ in plans],
        scratch_shapes=[pltpu.VMEM((tm, d), BF16)] + _weight_scratch(d, n),
        compiler_params=_params("arbitrary"),
        name="ret_in_proj",
    )(x2d, gain, w, cos, sin, *[view for view, _, _ in plans])
    return outs[0], [o.reshape(arr.shape) for o, arr in zip(outs[1:], side_weights)]


def _ret_core_kernel(q_ref, k_ref, v_ref, g_ref, dmat_ref, qd_ref, kd_ref, gn_ref, o_ref, state_ref, *, n_steps):
    dk, dv = RET_QK_DIM, RET_V_DIM
    state_ref[...] = jnp.zeros_like(state_ref)

    for sc in range(n_steps):
        rows = slice(sc * RET_SUPER, (sc + 1) * RET_SUPER)
        for hh in range(RET_HEADS_PER_STEP):
            qd = qd_ref[hh]
            kd = kd_ref[hh]
            step_decay = qd[RET_SUPER - 1:RET_SUPER, :]
            q = q_ref[0, rows, hh * dk:(hh + 1) * dk]
            k = k_ref[0, rows, hh * dk:(hh + 1) * dk]
            v = v_ref[0, rows, hh * dv:(hh + 1) * dv]
            scores = _dot_nt(q, k) * dmat_ref[hh]
            inner = _dot(scores.astype(BF16), v)
            state = state_ref[hh]
            q_scaled = (q.astype(F32) * qd).astype(BF16)
            cross = _dot(q_scaled, state.astype(BF16))
            k_scaled = (k.astype(F32) * kd).astype(BF16)
            state_ref[hh] = state * step_decay + _dot_tn(k_scaled, v)
            out = inner + cross
            out = out * _rms_scale(out) * gn_ref[hh]
            g = g_ref[0, rows, hh * dv:(hh + 1) * dv]
            o_ref[0, rows, hh * dv:(hh + 1) * dv] = out.astype(BF16) * _silu(g)


def _ret_core(qkvg, dmat, qd, kd, gn, batch, seq):
    hd, dk, dv = RET_HEADS, RET_QK_DIM, RET_V_DIM
    hps = RET_HEADS_PER_STEP
    groups = hd // hps
    qkvg3 = qkvg.reshape(batch, seq, qkvg.shape[-1])
    k_blk0 = groups
    v_blk0 = 2 * hd * dk // (hps * dv)
    g_blk0 = v_blk0 + groups
    kern = functools.partial(_ret_core_kernel, n_steps=seq // RET_SUPER)
    return pl.pallas_call(
        kern,
        grid=(batch, groups),
        in_specs=[
            pl.BlockSpec((1, seq, hps * dk), lambda b, h: (b, 0, h)),
            pl.BlockSpec((1, seq, hps * dk), lambda b, h: (b, 0, k_blk0 + h)),
            pl.BlockSpec((1, seq, hps * dv), lambda b, h: (b, 0, v_blk0 + h)),
            pl.BlockSpec((1, seq, hps * dv), lambda b, h: (b, 0, g_blk0 + h)),
            pl.BlockSpec((hps, RET_SUPER, RET_SUPER), lambda b, h: (h, 0, 0)),
            pl.BlockSpec((hps, RET_SUPER, 1), lambda b, h: (h, 0, 0)),
            pl.BlockSpec((hps, RET_SUPER, 1), lambda b, h: (h, 0, 0)),
            pl.BlockSpec((hps, 1, dv), lambda b, h: (h, 0, 0)),
        ],
        out_specs=pl.BlockSpec((1, seq, hps * dv), lambda b, h: (b, 0, h)),
        out_shape=jax.ShapeDtypeStruct((batch, seq, hd * dv), BF16),
        scratch_shapes=[pltpu.VMEM((hps, dk, dv), F32)],
        compiler_params=_params("parallel", "parallel"),
        name="ret_core",
    )(qkvg3, qkvg3, qkvg3, qkvg3, dmat, qd, kd, gn)


def _mix_ffn_kernel(a_ref, wo_ref, res_ref, gain_ref, w_in_ref, cw_ref, cb_ref, w_out_ref, o_ref,
                    h_ref, act_ref, carry_ref, *, tiles_per_seq):
    i = pl.program_id(0)
    tm = res_ref.shape[0]
    ffn_dim = w_out_ref.shape[0]

    x1 = res_ref[...] + _dot(a_ref[...], wo_ref[...])
    o_ref[...] = x1
    h_ref[...] = (x1 * _rms_scale(x1) * gain_ref[...]).astype(BF16)

    seq_start = lax.rem(i, tiles_per_seq) == 0
    row = lax.broadcasted_iota(jnp.int32, (tm, 1), 0)
    for c in range(ffn_dim // FFN_CHUNK):
        lo = c * FFN_CHUNK
        a = _dot(h_ref[...], w_in_ref[:, lo:lo + FFN_CHUNK])
        g = _dot(h_ref[...], w_in_ref[:, ffn_dim + lo:ffn_dim + lo + FFN_CHUNK])
        prev = jnp.where(seq_start, 0.0, carry_ref[:, lo:lo + FFN_CHUNK])
        prev1 = prev[SUBLANES - 1:SUBLANES, :]
        prev2 = prev[SUBLANES - 2:SUBLANES - 1, :]
        g1 = jnp.where(row == 0, prev1, pltpu.roll(g, 1, 0))
        g2 = jnp.where(row == 0, prev2, jnp.where(row == 1, prev1, pltpu.roll(g, 2, 0)))
        carry_ref[:, lo:lo + FFN_CHUNK] = g[tm - SUBLANES:tm, :]
        cw = cw_ref[:, lo:lo + FFN_CHUNK]
        gc = g2 * cw[0:1, :] + g1 * cw[1:2, :] + g * cw[2:3, :] + cb_ref[:, lo:lo + FFN_CHUNK]
        act_ref[:, lo:lo + FFN_CHUNK] = (_silu(gc) * a).astype(BF16)

    o_ref[...] += _dot(act_ref[...], w_out_ref[...])


def _resident_layer(arr, layer):
    return pl.BlockSpec((None,) + arr.shape[1:], lambda i: (layer,) + (0,) * (arr.ndim - 1),
                        pipeline_mode=pl.Buffered(1))


def _mix_ffn(a, w_o, res, gain, w_in, conv_w, conv_b, w_out, seq, layer):
    t, d = res.shape
    ka = a.shape[1]
    f = w_out.shape[1]
    tm = FFN_TM
    kern = functools.partial(_mix_ffn_kernel, tiles_per_seq=seq // tm)
    return pl.pallas_call(
        kern,
        grid=(t // tm,),
        in_specs=[
            pl.BlockSpec((tm, ka), lambda i: (i, 0)),
            _resident_layer(w_o, 0),
            pl.BlockSpec((tm, d), lambda i: (i, 0)),
            _resident(gain), _resident_layer(w_in, layer), _resident(conv_w), _resident(conv_b),
            _resident_layer(w_out, layer),
        ],
        out_specs=pl.BlockSpec((tm, d), lambda i: (i, 0)),
        out_shape=jax.ShapeDtypeStruct((t, d), F32),
        scratch_shapes=[pltpu.VMEM((tm, d), BF16), pltpu.VMEM((tm, f), BF16), pltpu.VMEM((SUBLANES, f), F32)],
        compiler_params=_params("arbitrary"),
        name="mix_ffn",
    )(a, w_o, res, gain, w_in, conv_w, conv_b, w_out)


def _mla_proj_kernel(x_ref, gain_ref, w_in_ref, qn_ref, kvn_ref, wq_ref, wkv_ref,
                     c_ref, s_ref, gq_ref, gk_ref, q_ref, k_ref, v_ref):
    x = x_ref[...]
    h = (x * _rms_scale(x) * gain_ref[...]).astype(BF16)
    p = _dot(h, w_in_ref[...])
    c_q = p[:, :MLA_Q_RANK]
    c_kv = p[:, MLA_Q_RANK:MLA_Q_RANK + MLA_KV_RANK]
    k_r = p[:, MLA_Q_RANK + MLA_KV_RANK:MLA_Q_RANK + MLA_KV_RANK + LANES]
    k_s = p[:, MLA_Q_RANK + MLA_KV_RANK + LANES:]
    c_q = (c_q * _rms_scale(c_q) * qn_ref[...]).astype(BF16)
    c_kv = (c_kv * _rms_scale(c_kv) * kvn_ref[...]).astype(BF16)

    cos = c_ref[...]
    sin = s_ref[...]
    gq_nope, gq_rope, gq_swap = gq_ref[0:1, :], gq_ref[1:2, :], gq_ref[2:3, :]
    gk_nope, gk_rope, gk_swap = gk_ref[0:1, :], gk_ref[1:2, :], gk_ref[2:3, :]
    inv_dim = 1.0 / MLA_QK_DIM
    q_scale = MLA_QK_DIM ** -0.5 * LOG2_E

    k_rope_sq = k_r * k_r
    k_rope_rot = k_r * gk_rope * cos + k_s * gk_swap * sin

    def project(pair):
        q2 = _dot(c_q, wq_ref[:, pair * 2 * MLA_Q_COLS:(pair + 1) * 2 * MLA_Q_COLS])
        kv2 = _dot(c_kv, wkv_ref[:, pair * 2 * MLA_HEAD_PAD:(pair + 1) * 2 * MLA_HEAD_PAD])
        return q2, kv2

    def finish(pair, q2, kv2):
        for sub in range(2):
            hh = 2 * pair + sub
            q_n = q2[:, sub * MLA_Q_COLS:sub * MLA_Q_COLS + LANES]
            q_r = q2[:, sub * MLA_Q_COLS + LANES:sub * MLA_Q_COLS + 2 * LANES]
            q_s = q2[:, sub * MLA_Q_COLS + 2 * LANES:(sub + 1) * MLA_Q_COLS]
            ss = jnp.sum(q_n * q_n + q_r * q_r, axis=-1, keepdims=True)
            r = lax.rsqrt(ss * inv_dim + RMS_EPS) * q_scale
            q_ref[0, hh, :, 0:LANES] = (q_n * r * gq_nope).astype(BF16)
            q_ref[0, hh, :, LANES:2 * LANES] = ((q_r * gq_rope * cos + q_s * gq_swap * sin) * r).astype(BF16)

            k_n = kv2[:, sub * MLA_HEAD_PAD:sub * MLA_HEAD_PAD + LANES]
            ssk = jnp.sum(k_n * k_n + k_rope_sq, axis=-1, keepdims=True)
            rk = lax.rsqrt(ssk * inv_dim + RMS_EPS)
            k_ref[0, hh, :, 0:LANES] = (k_n * rk * gk_nope).astype(BF16)
            k_ref[0, hh, :, LANES:2 * LANES] = (k_rope_rot * rk).astype(BF16)
            v_ref[0, hh] = kv2[:, sub * MLA_HEAD_PAD + LANES:(sub + 1) * MLA_HEAD_PAD].astype(BF16)

    n_pairs = MLA_HEADS // 2
    pending = project(0)
    for pair in range(n_pairs):
        upcoming = project(pair + 1) if pair + 1 < n_pairs else None
        finish(pair, *pending)
        pending = upcoming


def _mla_proj(x2d, gain, w_in, qn, kvn, wq, wkv, cos, sin, gq, gk, batch, seq):
    t, d = x2d.shape
    tm = MLA_TM
    tps = seq // tm
    hd = MLA_HEADS

    return pl.pallas_call(
        _mla_proj_kernel,
        grid=(t // tm,),
        in_specs=[
            pl.BlockSpec((tm, d), lambda i: (i, 0)),
            _resident(gain), _resident(w_in), _resident(qn), _resident(kvn), _resident(wq), _resident(wkv),
            pl.BlockSpec((tm, LANES), lambda i: (i % tps, 0)),
            pl.BlockSpec((tm, LANES), lambda i: (i % tps, 0)),
            _resident(gq), _resident(gk),
        ],
        out_specs=[
            pl.BlockSpec((1, hd, tm, MLA_HEAD_PAD), lambda i: (i // tps, 0, i % tps, 0)),
            pl.BlockSpec((1, hd, tm, MLA_HEAD_PAD), lambda i: (i // tps, 0, i % tps, 0)),
            pl.BlockSpec((1, hd, tm, MLA_V_DIM), lambda i: (i // tps, 0, i % tps, 0)),
        ],
        out_shape=[
            jax.ShapeDtypeStruct((batch, hd, seq, MLA_HEAD_PAD), BF16),
            jax.ShapeDtypeStruct((batch, hd, seq, MLA_HEAD_PAD), BF16),
            jax.ShapeDtypeStruct((batch, hd, seq, MLA_V_DIM), BF16),
        ],
        compiler_params=_params("parallel"),
        name="mla_proj",
    )(x2d, gain, w_in, qn, kvn, wq, wkv, cos, sin, gq, gk)


def _attn_kernel(q_ref, k_ref, v_ref, o_ref, vext_ref, kt_ref, *, n_q_blocks):
    tq = ATT_TQ
    row = lax.broadcasted_iota(jnp.int32, (tq, tq), 0) // CHUNK
    col = lax.broadcasted_iota(jnp.int32, (tq, tq), 1) // CHUNK
    visible = col <= row

    vext_ref[:, :, :MLA_V_DIM] = v_ref[0]
    vext_ref[:, :, MLA_V_DIM:] = jnp.ones(vext_ref.shape[:2] + (MLA_V_DIM,), BF16)

    for hh in range(ATT_HEADS_PER_STEP):
        kt_ref[hh] = k_ref[0, hh].T

    def scores(hh, qi):
        q0 = qi * tq
        q = q_ref[0, hh, q0:q0 + tq, :]
        s_diag = jnp.where(visible, _dot(q, kt_ref[hh, :, q0:q0 + tq]), MASK_VALUE)
        s_off = _dot(q, kt_ref[hh, :, 0:q0]) if qi > 0 else None
        return s_diag, s_off

    def finish(hh, qi, s_diag, s_off):
        q0 = qi * tq
        m = jnp.max(s_diag, axis=-1, keepdims=True)
        if qi > 0:
            m = jnp.maximum(m, jnp.max(s_off, axis=-1, keepdims=True))
        p_diag = jnp.exp2(s_diag - m)
        acc = _dot(p_diag.astype(BF16), vext_ref[hh, q0:q0 + tq, :])
        if qi > 0:
            p_off = jnp.exp2(s_off - m)
            acc = acc + _dot(p_off.astype(BF16), vext_ref[hh, 0:q0, :])
        out = acc[:, :MLA_V_DIM] / acc[:, MLA_V_DIM:]
        o_ref[0, q0:q0 + tq, hh * MLA_V_DIM:(hh + 1) * MLA_V_DIM] = out.astype(BF16)

    work = [(hh, qi) for hh in range(ATT_HEADS_PER_STEP) for qi in range(n_q_blocks - 1, -1, -1)]
    queue = [scores(*item) for item in work[:ATT_AHEAD]]
    for pos, item in enumerate(work):
        if pos + ATT_AHEAD < len(work):
            queue.append(scores(*work[pos + ATT_AHEAD]))
        finish(*item, *queue.pop(0))


def _attention(q, k, v):
    batch, hd, seq, _ = q.shape
    hps = ATT_HEADS_PER_STEP
    kern = functools.partial(_attn_kernel, n_q_blocks=seq // ATT_TQ)
    return pl.pallas_call(
        kern,
        grid=(batch, hd // hps),
        in_specs=[
            pl.BlockSpec((1, hps, seq, MLA_HEAD_PAD), lambda b, h: (b, h, 0, 0)),
            pl.BlockSpec((1, hps, seq, MLA_HEAD_PAD), lambda b, h: (b, h, 0, 0)),
            pl.BlockSpec((1, hps, seq, MLA_V_DIM), lambda b, h: (b, h, 0, 0)),
        ],
        out_specs=pl.BlockSpec((1, seq, hps * MLA_V_DIM), lambda b, h: (b, 0, h)),
        out_shape=jax.ShapeDtypeStruct((batch, seq, hd * MLA_V_DIM), BF16),
        scratch_shapes=[pltpu.VMEM((hps, seq, 2 * MLA_V_DIM), BF16), pltpu.VMEM((hps, MLA_HEAD_PAD, seq), BF16)],
        compiler_params=_params("parallel", "parallel"),
        name="mla_attention",
    )(q, k, v)


def _rope_tables(seq, half):
    inv_freq = ROPE_THETA ** (-jnp.arange(half, dtype=F32) / half)
    ang = jnp.arange(seq).astype(F32)[:, None] * inv_freq[None, :]
    return jnp.cos(ang), jnp.sin(ang)


def _retention_decay_tables():
    sc = RET_SUPER
    log_gamma = jnp.log1p(-jnp.exp2(RET_GAMMA_BASE - jnp.arange(RET_HEADS, dtype=F32)))
    idx = jnp.arange(sc, dtype=F32)
    dist = jnp.abs(idx[:, None] - idx[None, :])
    chunk = jnp.arange(sc) // CHUNK
    visible = chunk[None, :] <= chunk[:, None]
    dmat = jnp.where(visible[None], jnp.exp(log_gamma[:, None, None] * dist[None]), 0.0)
    qd = jnp.exp(log_gamma[:, None] * (idx + 1.0))[:, :, None]
    kd = jnp.exp(log_gamma[:, None] * (sc - 1.0 - idx))[:, :, None]
    return dmat, qd, kd


def _swap_halves(w):
    half = w.shape[-1] // 2
    return jnp.concatenate([w[..., half:], w[..., :half]], axis=-1)


def _pad_lanes(w):
    return jnp.pad(w, [(0, 0)] * (w.ndim - 1) + [(0, LANES - w.shape[-1])])


def _mla_weights(w_in, w_qb, q_head_g, k_head_g):
    rq, rkv = MLA_Q_RANK, MLA_KV_RANK
    w_kr = w_in[:, rq + rkv:]
    w_in_l = jnp.concatenate([w_in[:, :rq + rkv], _pad_lanes(w_kr), _pad_lanes(_swap_halves(w_kr))], axis=1)
    w_qb3 = w_qb.reshape(rq, MLA_HEADS, MLA_QK_DIM)
    nope, rope = w_qb3[..., :MLA_NOPE_DIM], w_qb3[..., MLA_NOPE_DIM:]
    wq = jnp.concatenate([nope, _pad_lanes(rope), _pad_lanes(_swap_halves(rope))], axis=-1)
    wq = wq.reshape(rq, MLA_HEADS * MLA_Q_COLS)

    def gains(g):
        rope_g = g[MLA_NOPE_DIM:]
        return jnp.stack([g[:MLA_NOPE_DIM], _pad_lanes(rope_g), _pad_lanes(_swap_halves(rope_g))])

    return w_in_l, wq, gains(q_head_g), gains(k_head_g)


def kernel(x, ret_norm, ret_w_in, ret_gn, ret_w_out, mla_norm, mla_w_in, mla_q_norm, mla_w_qb, mla_kv_norm,
           mla_w_kvb, mla_q_head_norm, mla_k_head_norm, mla_w_out, ffn_norm, ffn_w_in, ffn_conv_w, ffn_conv_b,
           ffn_w_out):
    batch, seq, d = x.shape
    x2d = x.reshape(batch * seq, d)

    cos_r, sin_r = _rope_tables(seq, RET_QK_DIM // 2)
    dmat, qd, kd = _retention_decay_tables()
    w_in_l, wq, gq, gk = _mla_weights(mla_w_in[0], mla_w_qb[0], mla_q_head_norm[0], mla_k_head_norm[0])
    side = [ret_w_out, ffn_w_in, ffn_w_out, w_in_l, wq, mla_w_kvb[0], mla_w_out]
    qkvg, (ret_w_out_b, ffn_w_in_b, ffn_w_out_b, w_in_l_b, wq_b, wkv_b, mla_w_out_b) = _ret_in_proj(
        x2d, ret_norm[0][None, :], ret_w_in, cos_r, sin_r, seq, side)
    ret = _ret_core(qkvg, dmat, qd, kd, ret_gn[0][:, None, :], batch, seq)
    x2d = _mix_ffn(ret.reshape(batch * seq, -1), ret_w_out_b, x2d, ffn_norm[0][None, :],
                   ffn_w_in_b, ffn_conv_w[0], ffn_conv_b[0][None, :], ffn_w_out_b, seq, layer=0)

    cos_m, sin_m = _rope_tables(seq, MLA_ROPE_DIM // 2)
    pad = jnp.zeros((seq, LANES - MLA_ROPE_DIM), F32)
    cos_t = jnp.concatenate([cos_m, cos_m, pad], axis=1)
    sin_t = jnp.concatenate([-sin_m, sin_m, pad], axis=1)
    q, k, v = _mla_proj(x2d, mla_norm[0][None, :], w_in_l_b, mla_q_norm[0][None, :],
                        mla_kv_norm[0][None, :], wq_b, wkv_b, cos_t, sin_t, gq, gk, batch, seq)
    att = _attention(q, k, v)
    x2d = _mix_ffn(att.reshape(batch * seq, -1), mla_w_out_b, x2d, ffn_norm[1][None, :],
                   ffn_w_in_b, ffn_conv_w[1], ffn_conv_b[1][None, :], ffn_w_out_b, seq, layer=1)
    return x2d.reshape(batch, seq, d)
```

```python
import functools

import jax
import jax.numpy as jnp
from jax import lax
from jax.experimental import pallas as pl
from jax.experimental.pallas import tpu as pltpu

F32 = jnp.float32
BF16 = jnp.bfloat16

CHUNK = 64
RMS_EPS = 1e-6
ROPE_THETA = 10000.0
RET_HEADS = 4
RET_QK_DIM = 256
RET_V_DIM = 512
RET_GAMMA_BASE = -5.0
MLA_HEADS = 8
MLA_Q_RANK = 384
MLA_KV_RANK = 256
MLA_NOPE_DIM = 128
MLA_ROPE_DIM = 64
MLA_V_DIM = 128
MLA_QK_DIM = MLA_NOPE_DIM + MLA_ROPE_DIM
MLA_HEAD_PAD = 256
MLA_Q_COLS = 3 * 128
MASK_VALUE = -1e30
CONV_WIDTH = 3
LOG2_E = 1.4426950408889634

LANES = 128
SUBLANES = 8
VMEM_LIMIT_BYTES = 58 * 1024 * 1024
BF16_ROWS = 16

RET_SUPER = 256
PROJ_TM = 512
RET_HEADS_PER_STEP = 2
FFN_TM = 1024
FFN_CHUNK = 256
MLA_TM = 1024
W_STAGE_BYTES = 384 * 1024
W_STAGE_SLOTS = 4
ATT_TQ = 256
ATT_HEADS_PER_STEP = 2
ATT_AHEAD = 2


def _params(*sem):
    return pltpu.CompilerParams(dimension_semantics=sem, vmem_limit_bytes=VMEM_LIMIT_BYTES)


def _resident(arr):
    return pl.BlockSpec(arr.shape, lambda i: (0,) * arr.ndim, pipeline_mode=pl.Buffered(1))


def _stage_rows(rows, cols):
    limit = W_STAGE_BYTES // (cols * 4)
    fits = [r for r in range(BF16_ROWS, limit + 1, BF16_ROWS) if rows % r == 0]
    return max(fits)


def _weight_scratch(rows, cols):
    return [pltpu.VMEM((rows, cols), BF16), pltpu.VMEM((W_STAGE_SLOTS, _stage_rows(rows, cols), cols), F32),
            pltpu.SemaphoreType.DMA((W_STAGE_SLOTS,))]


def _load_weight(src_ref, dst_ref, stage_ref, sem_ref):
    slots, rows = stage_ref.shape[0], stage_ref.shape[1]
    n_chunks = dst_ref.shape[0] // rows
    lookahead = slots - 1
    assert n_chunks >= lookahead

    def copy(k, slot):
        return pltpu.make_async_copy(src_ref.at[pl.ds(k * rows, rows), :], stage_ref.at[slot], sem_ref.at[slot])

    for k in range(lookahead):
        copy(k, k).start()

    def body(k, carry):
        nxt = k + lookahead

        @pl.when(nxt < n_chunks)
        def _():
            copy(nxt, lax.rem(nxt, slots)).start()

        slot = lax.rem(k, slots)
        copy(k, slot).wait()
        r0 = pl.multiple_of(k * rows, rows)
        dst_ref[pl.ds(r0, rows), :] = stage_ref[slot].astype(BF16)
        return carry

    lax.fori_loop(0, n_chunks, body, 0)


def _rms_scale(x):
    return lax.rsqrt(jnp.mean(x * x, axis=-1, keepdims=True) + RMS_EPS)


def _silu(x):
    half = 0.5 * x
    return half + half * jnp.tanh(half)


def _dot(a, b):
    return jnp.dot(a, b, preferred_element_type=F32)


def _dot_nt(a, b):
    return lax.dot_general(a, b, (((1,), (1,)), ((), ())), preferred_element_type=F32)


def _dot_tn(a, b):
    return lax.dot_general(a, b, (((0,), (0,)), ((), ())), preferred_element_type=F32)


def _ret_in_kernel(*refs, n_side):
    x_ref, gain_ref, w_hbm, cos_ref, sin_ref = refs[:5]
    side_in = refs[5:5 + n_side]
    o_ref = refs[5 + n_side]
    side_out = refs[6 + n_side:6 + 2 * n_side]
    h_ref, w_ref, stage_ref, sem_ref = refs[6 + 2 * n_side:]
    i = pl.program_id(0)
    half = RET_QK_DIM // 2

    @pl.when(i == 0)
    def _():
        _load_weight(w_hbm.at[0], w_ref, stage_ref, sem_ref)

    for src_ref, dst_ref in zip(side_in, side_out):
        dst_ref[...] = src_ref[...].astype(BF16)

    x = x_ref[...]
    h_ref[...] = (x * _rms_scale(x) * gain_ref[...]).astype(BF16)
    cos = cos_ref[...]
    sin = sin_ref[...]
    k_scale = RET_QK_DIM ** -0.5
    cos_k = cos * k_scale
    sin_k = sin * k_scale
    for c in range(w_ref.shape[1] // RET_QK_DIM):
        lo = c * RET_QK_DIM
        acc = _dot(h_ref[...], w_ref[:, lo:lo + RET_QK_DIM])
        if c < 2 * RET_HEADS:
            cs, sn = (cos, sin) if c < RET_HEADS else (cos_k, sin_k)
            x1 = acc[:, :half]
            x2 = acc[:, half:]
            o_ref[:, lo:lo + half] = (x1 * cs - x2 * sn).astype(BF16)
            o_ref[:, lo + half:lo + RET_QK_DIM] = (x2 * cs + x1 * sn).astype(BF16)
        else:
            o_ref[:, lo:lo + RET_QK_DIM] = acc.astype(BF16)


def _cast_plan(arr, n_grid):
    cols = arr.shape[-1]
    rows = arr.size // cols
    steps = max(s for s in range(1, n_grid + 1) if rows % s == 0 and (rows // s) % BF16_ROWS == 0)
    return arr.reshape(rows, cols), steps, rows // steps


def _ret_in_proj(x2d, gain, w, cos, sin, seq, side_weights):
    t, d = x2d.shape
    n = w.shape[-1]
    tm = PROJ_TM
    tps = seq // tm
    n_grid = t // tm
    plans = [_cast_plan(arr, n_grid) for arr in side_weights]
    side_specs = [pl.BlockSpec((rps, view.shape[1]), lambda i, last=steps - 1: (jnp.minimum(i, last), 0))
                  for view, steps, rps in plans]
    kern = functools.partial(_ret_in_kernel, n_side=len(plans))
    outs = pl.pallas_call(
        kern,
        grid=(n_grid,),
        in_specs=[
            pl.BlockSpec((tm, d), lambda i: (i, 0)),
            _resident(gain), pl.BlockSpec(memory_space=pl.ANY),
            pl.BlockSpec((tm, RET_QK_DIM // 2), lambda i: (i % tps, 0)),
            pl.BlockSpec((tm, RET_QK_DIM // 2), lambda i: (i % tps, 0)),
        ] + side_specs,
        out_specs=[pl.BlockSpec((tm, n), lambda i: (i, 0))] + side_specs,
        out_shape=[jax.ShapeDtypeStruct((t, n), BF16)]
        + [jax.ShapeDtypeStruct(view.shape, BF16) for view, _, _ in plans],
        scratch_shapes=[pltpu.VMEM((tm, d), BF16)] + _weight_scratch(d, n),
        compiler_params=_params("arbitrary"),
        name="ret_in_proj",
    )(x2d, gain, w, cos, sin, *[view for view, _, _ in plans])
    return outs[0], [o.reshape(arr.shape) for o, arr in zip(outs[1:], side_weights)]


def _ret_core_kernel(q_ref, k_ref, v_ref, g_ref, dmat_ref, qd_ref, kd_ref, gn_ref, o_ref, state_ref, *, n_steps):
    dk, dv = RET_QK_DIM, RET_V_DIM
    state_ref[...] = jnp.zeros_like(state_ref)

    for sc in range(n_steps):
        rows = slice(sc * RET_SUPER, (sc + 1) * RET_SUPER)
        for hh in range(RET_HEADS_PER_STEP):
            qd = qd_ref[hh]
            kd = kd_ref[hh]
            step_decay = qd[RET_SUPER - 1:RET_SUPER, :]
            q = q_ref[0, rows, hh * dk:(hh + 1) * dk]
            k = k_ref[0, rows, hh * dk:(hh + 1) * dk]
            v = v_ref[0, rows, hh * dv:(hh + 1) * dv]
            scores = _dot_nt(q, k) * dmat_ref[hh]
            inner = _dot(scores.astype(BF16), v)
            state = state_ref[hh]
            q_scaled = (q.astype(F32) * qd).astype(BF16)
            cross = _dot(q_scaled, state.astype(BF16))
            k_scaled = (k.astype(F32) * kd).astype(BF16)
            state_ref[hh] = state * step_decay + _dot_tn(k_scaled, v)
            out = inner + cross
            out = out * _rms_scale(out) * gn_ref[hh]
            g = g_ref[0, rows, hh * dv:(hh + 1) * dv]
            o_ref[0, rows, hh * dv:(hh + 1) * dv] = out.astype(BF16) * _silu(g)


def _ret_core(qkvg, dmat, qd, kd, gn, batch, seq):
    hd, dk, dv = RET_HEADS, RET_QK_DIM, RET_V_DIM
    hps = RET_HEADS_PER_STEP
    groups = hd // hps
    qkvg3 = qkvg.reshape(batch, seq, qkvg.shape[-1])
    k_blk0 = groups
    v_blk0 = 2 * hd * dk // (hps * dv)
    g_blk0 = v_blk0 + groups
    kern = functools.partial(_ret_core_kernel, n_steps=seq // RET_SUPER)
    return pl.pallas_call(
        kern,
        grid=(batch, groups),
        in_specs=[
            pl.BlockSpec((1, seq, hps * dk), lambda b, h: (b, 0, h)),
            pl.BlockSpec((1, seq, hps * dk), lambda b, h: (b, 0, k_blk0 + h)),
            pl.BlockSpec((1, seq, hps * dv), lambda b, h: (b, 0, v_blk0 + h)),
            pl.BlockSpec((1, seq, hps * dv), lambda b, h: (b, 0, g_blk0 + h)),
            pl.BlockSpec((hps, RET_SUPER, RET_SUPER), lambda b, h: (h, 0, 0)),
            pl.BlockSpec((hps, RET_SUPER, 1), lambda b, h: (h, 0, 0)),
            pl.BlockSpec((hps, RET_SUPER, 1), lambda b, h: (h, 0, 0)),
            pl.BlockSpec((hps, 1, dv), lambda b, h: (h, 0, 0)),
        ],
        out_specs=pl.BlockSpec((1, seq, hps * dv), lambda b, h: (b, 0, h)),
        out_shape=jax.ShapeDtypeStruct((batch, seq, hd * dv), BF16),
        scratch_shapes=[pltpu.VMEM((hps, dk, dv), F32)],
        compiler_params=_params("parallel", "parallel"),
        name="ret_core",
    )(qkvg3, qkvg3, qkvg3, qkvg3, dmat, qd, kd, gn)


def _mix_ffn_kernel(a_ref, wo_ref, res_ref, gain_ref, w_in_ref, cw_ref, cb_ref, w_out_ref, o_ref,
                    h_ref, act_ref, carry_ref, *, tiles_per_seq):
    i = pl.program_id(0)
    tm = res_ref.shape[0]
    ffn_dim = w_out_ref.shape[0]

    x1 = res_ref[...] + _dot(a_ref[...], wo_ref[...])
    o_ref[...] = x1
    h_ref[...] = (x1 * _rms_scale(x1) * gain_ref[...]).astype(BF16)

    seq_start = lax.rem(i, tiles_per_seq) == 0
    row = lax.broadcasted_iota(jnp.int32, (tm, 1), 0)
    for c in range(ffn_dim // FFN_CHUNK):
        lo = c * FFN_CHUNK
        a = _dot(h_ref[...], w_in_ref[:, lo:lo + FFN_CHUNK])
        g = _dot(h_ref[...], w_in_ref[:, ffn_dim + lo:ffn_dim + lo + FFN_CHUNK])
        prev = jnp.where(seq_start, 0.0, carry_ref[:, lo:lo + FFN_CHUNK])
        prev1 = prev[SUBLANES - 1:SUBLANES, :]
        prev2 = prev[SUBLANES - 2:SUBLANES - 1, :]
        g1 = jnp.where(row == 0, prev1, pltpu.roll(g, 1, 0))
        g2 = jnp.where(row == 0, prev2, jnp.where(row == 1, prev1, pltpu.roll(g, 2, 0)))
        carry_ref[:, lo:lo + FFN_CHUNK] = g[tm - SUBLANES:tm, :]
        cw = cw_ref[:, lo:lo + FFN_CHUNK]
        gc = g2 * cw[0:1, :] + g1 * cw[1:2, :] + g * cw[2:3, :] + cb_ref[:, lo:lo + FFN_CHUNK]
        act_ref[:, lo:lo + FFN_CHUNK] = (_silu(gc) * a).astype(BF16)

    o_ref[...] += _dot(act_ref[...], w_out_ref[...])


def _resident_layer(arr, layer):
    return pl.BlockSpec((None,) + arr.shape[1:], lambda i: (layer,) + (0,) * (arr.ndim - 1),
                        pipeline_mode=pl.Buffered(1))


def _mix_ffn(a, w_o, res, gain, w_in, conv_w, conv_b, w_out, seq, layer):
    t, d = res.shape
    ka = a.shape[1]
    f = w_out.shape[1]
    tm = FFN_TM
    kern = functools.partial(_mix_ffn_kernel, tiles_per_seq=seq // tm)
    return pl.pallas_call(
        kern,
        grid=(t // tm,),
        in_specs=[
            pl.BlockSpec((tm, ka), lambda i: (i, 0)),
            _resident_layer(w_o, 0),
            pl.BlockSpec((tm, d), lambda i: (i, 0)),
            _resident(gain), _resident_layer(w_in, layer), _resident(conv_w), _resident(conv_b),
            _resident_layer(w_out, layer),
        ],
        out_specs=pl.BlockSpec((tm, d), lambda i: (i, 0)),
        out_shape=jax.ShapeDtypeStruct((t, d), F32),
        scratch_shapes=[pltpu.VMEM((tm, d), BF16), pltpu.VMEM((tm, f), BF16), pltpu.VMEM((SUBLANES, f), F32)],
        compiler_params=_params("arbitrary"),
        name="mix_ffn",
    )(a, w_o, res, gain, w_in, conv_w, conv_b, w_out)


def _mla_proj_kernel(x_ref, gain_ref, w_in_ref, qn_ref, kvn_ref, wq_ref, wkv_ref,
                     c_ref, s_ref, gq_ref, gk_ref, q_ref, k_ref, v_ref):
    x = x_ref[...]
    h = (x * _rms_scale(x) * gain_ref[...]).astype(BF16)
    p = _dot(h, w_in_ref[...])
    c_q = p[:, :MLA_Q_RANK]
    c_kv = p[:, MLA_Q_RANK:MLA_Q_RANK + MLA_KV_RANK]
    k_r = p[:, MLA_Q_RANK + MLA_KV_RANK:MLA_Q_RANK + MLA_KV_RANK + LANES]
    k_s = p[:, MLA_Q_RANK + MLA_KV_RANK + LANES:]
    c_q = (c_q * _rms_scale(c_q) * qn_ref[...]).astype(BF16)
    c_kv = (c_kv * _rms_scale(c_kv) * kvn_ref[...]).astype(BF16)

    cos = c_ref[...]
    sin = s_ref[...]
    gq_nope, gq_rope, gq_swap = gq_ref[0:1, :], gq_ref[1:2, :], gq_ref[2:3, :]
    gk_nope, gk_rope, gk_swap = gk_ref[0:1, :], gk_ref[1:2, :], gk_ref[2:3, :]
    inv_dim = 1.0 / MLA_QK_DIM
    q_scale = MLA_QK_DIM ** -0.5 * LOG2_E

    k_rope_sq = k_r * k_r
    k_rope_rot = k_r * gk_rope * cos + k_s * gk_swap * sin

    def project(pair):
        q2 = _dot(c_q, wq_ref[:, pair * 2 * MLA_Q_COLS:(pair + 1) * 2 * MLA_Q_COLS])
        kv2 = _dot(c_kv, wkv_ref[:, pair * 2 * MLA_HEAD_PAD:(pair + 1) * 2 * MLA_HEAD_PAD])
        return q2, kv2

    def finish(pair, q2, kv2):
        for sub in range(2):
            hh = 2 * pair + sub
            q_n = q2[:, sub * MLA_Q_COLS:sub * MLA_Q_COLS + LANES]
            q_r = q2[:, sub * MLA_Q_COLS + LANES:sub * MLA_Q_COLS + 2 * LANES]
            q_s = q2[:, sub * MLA_Q_COLS + 2 * LANES:(sub + 1) * MLA_Q_COLS]
            ss = jnp.sum(q_n * q_n + q_r * q_r, axis=-1, keepdims=True)
            r = lax.rsqrt(ss * inv_dim + RMS_EPS) * q_scale
            q_ref[0, hh, :, 0:LANES] = (q_n * r * gq_nope).astype(BF16)
            q_ref[0, hh, :, LANES:2 * LANES] = ((q_r * gq_rope * cos + q_s * gq_swap * sin) * r).astype(BF16)

            k_n = kv2[:, sub * MLA_HEAD_PAD:sub * MLA_HEAD_PAD + LANES]
            ssk = jnp.sum(k_n * k_n + k_rope_sq, axis=-1, keepdims=True)
            rk = lax.rsqrt(ssk * inv_dim + RMS_EPS)
            k_ref[0, hh, :, 0:LANES] = (k_n * rk * gk_nope).astype(BF16)
            k_ref[0, hh, :, LANES:2 * LANES] = (k_rope_rot * rk).astype(BF16)
            v_ref[0, hh] = kv2[:, sub * MLA_HEAD_PAD + LANES:(sub + 1) * MLA_HEAD_PAD].astype(BF16)

    n_pairs = MLA_HEADS // 2
    pending = project(0)
    for pair in range(n_pairs):
        upcoming = project(pair + 1) if pair + 1 < n_pairs else None
        finish(pair, *pending)
        pending = upcoming


def _mla_proj(x2d, gain, w_in, qn, kvn, wq, wkv, cos, sin, gq, gk, batch, seq):
    t, d = x2d.shape
    tm = MLA_TM
    tps = seq // tm
    hd = MLA_HEADS

    return pl.pallas_call(
        _mla_proj_kernel,
        grid=(t // tm,),
        in_specs=[
            pl.BlockSpec((tm, d), lambda i: (i, 0)),
            _resident(gain), _resident(w_in), _resident(qn), _resident(kvn), _resident(wq), _resident(wkv),
            pl.BlockSpec((tm, LANES), lambda i: (i % tps, 0)),
            pl.BlockSpec((tm, LANES), lambda i: (i % tps, 0)),
            _resident(gq), _resident(gk),
        ],
        out_specs=[
            pl.BlockSpec((1, hd, tm, MLA_HEAD_PAD), lambda i: (i // tps, 0, i % tps, 0)),
            pl.BlockSpec((1, hd, tm, MLA_HEAD_PAD), lambda i: (i // tps, 0, i % tps, 0)),
            pl.BlockSpec((1, hd, tm, MLA_V_DIM), lambda i: (i // tps, 0, i % tps, 0)),
        ],
        out_shape=[
            jax.ShapeDtypeStruct((batch, hd, seq, MLA_HEAD_PAD), BF16),
            jax.ShapeDtypeStruct((batch, hd, seq, MLA_HEAD_PAD), BF16),
            jax.ShapeDtypeStruct((batch, hd, seq, MLA_V_DIM), BF16),
        ],
        compiler_params=_params("parallel"),
        name="mla_proj",
    )(x2d, gain, w_in, qn, kvn, wq, wkv, cos, sin, gq, gk)


def _attn_kernel(q_ref, k_ref, v_ref, o_ref, vext_ref, *, n_q_blocks):
    tq = ATT_TQ
    row = lax.broadcasted_iota(jnp.int32, (tq, tq), 0) // CHUNK
    col = lax.broadcasted_iota(jnp.int32, (tq, tq), 1) // CHUNK
    visible = col <= row

    vext_ref[:, :, :MLA_V_DIM] = v_ref[0]
    vext_ref[:, :, MLA_V_DIM:] = jnp.ones(vext_ref.shape[:2] + (MLA_V_DIM,), BF16)

    def scores(hh, qi):
        q0 = qi * tq
        q = q_ref[0, hh, q0:q0 + tq, :]
        s_diag = jnp.where(visible, _dot_nt(q, k_ref[0, hh, q0:q0 + tq, :]), MASK_VALUE)
        s_off = _dot_nt(q, k_ref[0, hh, 0:q0, :]) if qi > 0 else None
        return s_diag, s_off

    def finish(hh, qi, s_diag, s_off):
        q0 = qi * tq
        m = jnp.max(s_diag, axis=-1, keepdims=True)
        if qi > 0:
            m = jnp.maximum(m, jnp.max(s_off, axis=-1, keepdims=True))
        p_diag = jnp.exp2(s_diag - m)
        acc = _dot(p_diag.astype(BF16), vext_ref[hh, q0:q0 + tq, :])
        if qi > 0:
            p_off = jnp.exp2(s_off - m)
            acc = acc + _dot(p_off.astype(BF16), vext_ref[hh, 0:q0, :])
        out = acc[:, :MLA_V_DIM] / acc[:, MLA_V_DIM:]
        o_ref[0, q0:q0 + tq, hh * MLA_V_DIM:(hh + 1) * MLA_V_DIM] = out.astype(BF16)

    work = [(hh, qi) for hh in range(ATT_HEADS_PER_STEP) for qi in range(n_q_blocks - 1, -1, -1)]
    queue = [scores(*item) for item in work[:ATT_AHEAD]]
    for pos, item in enumerate(work):
        if pos + ATT_AHEAD < len(work):
            queue.append(scores(*work[pos + ATT_AHEAD]))
        finish(*item, *queue.pop(0))


def _attention(q, k, v):
    batch, hd, seq, _ = q.shape
    hps = ATT_HEADS_PER_STEP
    kern = functools.partial(_attn_kernel, n_q_blocks=seq // ATT_TQ)
    return pl.pallas_call(
        kern,
        grid=(batch, hd // hps),
        in_specs=[
            pl.BlockSpec((1, hps, seq, MLA_HEAD_PAD), lambda b, h: (b, h, 0, 0)),
            pl.BlockSpec((1, hps, seq, MLA_HEAD_PAD), lambda b, h: (b, h, 0, 0)),
            pl.BlockSpec((1, hps, seq, MLA_V_DIM), lambda b, h: (b, h, 0, 0)),
        ],
        out_specs=pl.BlockSpec((1, seq, hps * MLA_V_DIM), lambda b, h: (b, 0, h)),
        out_shape=jax.ShapeDtypeStruct((batch, seq, hd * MLA_V_DIM), BF16),
        scratch_shapes=[pltpu.VMEM((hps, seq, 2 * MLA_V_DIM), BF16)],
        compiler_params=_params("parallel", "parallel"),
        name="mla_attention",
    )(q, k, v)


def _rope_tables(seq, half):
    inv_freq = ROPE_THETA ** (-jnp.arange(half, dtype=F32) / half)
    ang = jnp.arange(seq).astype(F32)[:, None] * inv_freq[None, :]
    return jnp.cos(ang), jnp.sin(ang)


def _retention_decay_tables():
    sc = RET_SUPER
    log_gamma = jnp.log1p(-jnp.exp2(RET_GAMMA_BASE - jnp.arange(RET_HEADS, dtype=F32)))
    idx = jnp.arange(sc, dtype=F32)
    dist = jnp.abs(idx[:, None] - idx[None, :])
    chunk = jnp.arange(sc) // CHUNK
    visible = chunk[None, :] <= chunk[:, None]
    dmat = jnp.where(visible[None], jnp.exp(log_gamma[:, None, None] * dist[None]), 0.0)
    qd = jnp.exp(log_gamma[:, None] * (idx + 1.0))[:, :, None]
    kd = jnp.exp(log_gamma[:, None] * (sc - 1.0 - idx))[:, :, None]
    return dmat, qd, kd


def _swap_halves(w):
    half = w.shape[-1] // 2
    return jnp.concatenate([w[..., half:], w[..., :half]], axis=-1)


def _pad_lanes(w):
    return jnp.pad(w, [(0, 0)] * (w.ndim - 1) + [(0, LANES - w.shape[-1])])


def _mla_weights(w_in, w_qb, q_head_g, k_head_g):
    rq, rkv = MLA_Q_RANK, MLA_KV_RANK
    w_kr = w_in[:, rq + rkv:]
    w_in_l = jnp.concatenate([w_in[:, :rq + rkv], _pad_lanes(w_kr), _pad_lanes(_swap_halves(w_kr))], axis=1)
    w_qb3 = w_qb.reshape(rq, MLA_HEADS, MLA_QK_DIM)
    nope, rope = w_qb3[..., :MLA_NOPE_DIM], w_qb3[..., MLA_NOPE_DIM:]
    wq = jnp.concatenate([nope, _pad_lanes(rope), _pad_lanes(_swap_halves(rope))], axis=-1)
    wq = wq.reshape(rq, MLA_HEADS * MLA_Q_COLS)

    def gains(g):
        rope_g = g[MLA_NOPE_DIM:]
        return jnp.stack([g[:MLA_NOPE_DIM], _pad_lanes(rope_g), _pad_lanes(_swap_halves(rope_g))])

    return w_in_l, wq, gains(q_head_g), gains(k_head_g)


def kernel(x, ret_norm, ret_w_in, ret_gn, ret_w_out, mla_norm, mla_w_in, mla_q_norm, mla_w_qb, mla_kv_norm,
           mla_w_kvb, mla_q_head_norm, mla_k_head_norm, mla_w_out, ffn_norm, ffn_w_in, ffn_conv_w, ffn_conv_b,
           ffn_w_out):
    batch, seq, d = x.shape
    x2d = x.reshape(batch * seq, d)

    cos_r, sin_r = _rope_tables(seq, RET_QK_DIM // 2)
    dmat, qd, kd = _retention_decay_tables()
    w_in_l, wq, gq, gk = _mla_weights(mla_w_in[0], mla_w_qb[0], mla_q_head_norm[0], mla_k_head_norm[0])
    side = [ret_w_out, ffn_w_in, ffn_w_out, w_in_l, wq, mla_w_kvb[0], mla_w_out]
    qkvg, (ret_w_out_b, ffn_w_in_b, ffn_w_out_b, w_in_l_b, wq_b, wkv_b, mla_w_out_b) = _ret_in_proj(
        x2d, ret_norm[0][None, :], ret_w_in, cos_r, sin_r, seq, side)
    ret = _ret_core(qkvg, dmat, qd, kd, ret_gn[0][:, None, :], batch, seq)
    x2d = _mix_ffn(ret.reshape(batch * seq, -1), ret_w_out_b, x2d, ffn_norm[0][None, :],
                   ffn_w_in_b, ffn_conv_w[0], ffn_conv_b[0][None, :], ffn_w_out_b, seq, layer=0)

    cos_m, sin_m = _rope_tables(seq, MLA_ROPE_DIM // 2)
    pad = jnp.zeros((seq, LANES - MLA_ROPE_DIM), F32)
    cos_t = jnp.concatenate([cos_m, cos_m, pad], axis=1)
    sin_t = jnp.concatenate([-sin_m, sin_m, pad], axis=1)
    q, k, v = _mla_proj(x2d, mla_norm[0][None, :], w_in_l_b, mla_q_norm[0][None, :],
                        mla_kv_norm[0][None, :], wq_b, wkv_b, cos_t, sin_t, gq, gk, batch, seq)
    att = _attention(q, k, v)
    x2d = _mix_ffn(att.reshape(batch * seq, -1), mla_w_out_b, x2d, ffn_norm[1][None, :],
                   ffn_w_in_b, ffn_conv_w[1], ffn_conv_b[1][None, :], ffn_w_out_b, seq, layer=1)
    return x2d.reshape(batch, seq, d)
```

```python
import functools

import jax
import jax.numpy as jnp
from jax import lax
from jax.experimental import pallas as pl
from jax.experimental.pallas import tpu as pltpu

F32 = jnp.float32
BF16 = jnp.bfloat16

CHUNK = 64
RMS_EPS = 1e-6
ROPE_THETA = 10000.0
RET_HEADS = 4
RET_QK_DIM = 256
RET_V_DIM = 512
RET_GAMMA_BASE = -5.0
MLA_HEADS = 8
MLA_Q_RANK = 384
MLA_KV_RANK = 256
MLA_NOPE_DIM = 128
MLA_ROPE_DIM = 64
MLA_V_DIM = 128
MLA_QK_DIM = MLA_NOPE_DIM + MLA_ROPE_DIM
MLA_HEAD_PAD = 256
MLA_Q_COLS = 3 * 128
MASK_VALUE = -1e30
CONV_WIDTH = 3
LOG2_E = 1.4426950408889634

LANES = 128
SUBLANES = 8
VMEM_LIMIT_BYTES = 58 * 1024 * 1024
BF16_ROWS = 16

RET_SUPER = 256
PROJ_TM = 512
RET_HEADS_PER_STEP = 2
FFN_TM = 1024
FFN_CHUNK = 256
MLA_TM = 1024
W_STAGE_BYTES = 384 * 1024
W_STAGE_SLOTS = 4
ATT_TQ = 256
ATT_HEADS_PER_STEP = 2
ATT_AHEAD = 2


def _params(*sem):
    return pltpu.CompilerParams(dimension_semantics=sem, vmem_limit_bytes=VMEM_LIMIT_BYTES)


def _resident(arr):
    return pl.BlockSpec(arr.shape, lambda i: (0,) * arr.ndim, pipeline_mode=pl.Buffered(1))


def _stage_rows(rows, cols):
    limit = W_STAGE_BYTES // (cols * 4)
    fits = [r for r in range(BF16_ROWS, limit + 1, BF16_ROWS) if rows % r == 0]
    return max(fits)


def _weight_scratch(rows, cols):
    return [pltpu.VMEM((rows, cols), BF16), pltpu.VMEM((W_STAGE_SLOTS, _stage_rows(rows, cols), cols), F32),
            pltpu.SemaphoreType.DMA((W_STAGE_SLOTS,))]


def _load_weight(src_ref, dst_ref, stage_ref, sem_ref):
    slots, rows = stage_ref.shape[0], stage_ref.shape[1]
    n_chunks = dst_ref.shape[0] // rows
    lookahead = slots - 1
    assert n_chunks >= lookahead

    def copy(k, slot):
        return pltpu.make_async_copy(src_ref.at[pl.ds(k * rows, rows), :], stage_ref.at[slot], sem_ref.at[slot])

    for k in range(lookahead):
        copy(k, k).start()

    def body(k, carry):
        nxt = k + lookahead

        @pl.when(nxt < n_chunks)
        def _():
            copy(nxt, lax.rem(nxt, slots)).start()

        slot = lax.rem(k, slots)
        copy(k, slot).wait()
        r0 = pl.multiple_of(k * rows, rows)
        dst_ref[pl.ds(r0, rows), :] = stage_ref[slot].astype(BF16)
        return carry

    lax.fori_loop(0, n_chunks, body, 0)


def _rms_scale(x):
    return lax.rsqrt(jnp.mean(x * x, axis=-1, keepdims=True) + RMS_EPS)


def _silu(x):
    half = 0.5 * x
    return half + half * jnp.tanh(half)


def _dot(a, b):
    return jnp.dot(a, b, preferred_element_type=F32)


def _dot_nt(a, b):
    return lax.dot_general(a, b, (((1,), (1,)), ((), ())), preferred_element_type=F32)


def _dot_tn(a, b):
    return lax.dot_general(a, b, (((0,), (0,)), ((), ())), preferred_element_type=F32)


def _ret_in_kernel(*refs, n_side):
    x_ref, gain_ref, w_hbm, cos_ref, sin_ref = refs[:5]
    side_in = refs[5:5 + n_side]
    o_ref = refs[5 + n_side]
    side_out = refs[6 + n_side:6 + 2 * n_side]
    h_ref, w_ref, stage_ref, sem_ref = refs[6 + 2 * n_side:]
    i = pl.program_id(0)
    half = RET_QK_DIM // 2

    @pl.when(i == 0)
    def _():
        _load_weight(w_hbm.at[0], w_ref, stage_ref, sem_ref)

    for src_ref, dst_ref in zip(side_in, side_out):
        dst_ref[...] = src_ref[...].astype(BF16)

    x = x_ref[...]
    h_ref[...] = (x * _rms_scale(x) * gain_ref[...]).astype(BF16)
    cos = cos_ref[...]
    sin = sin_ref[...]
    k_scale = RET_QK_DIM ** -0.5
    cos_k = cos * k_scale
    sin_k = sin * k_scale
    for c in range(w_ref.shape[1] // RET_QK_DIM):
        lo = c * RET_QK_DIM
        acc = _dot(h_ref[...], w_ref[:, lo:lo + RET_QK_DIM])
        if c < 2 * RET_HEADS:
            cs, sn = (cos, sin) if c < RET_HEADS else (cos_k, sin_k)
            x1 = acc[:, :half]
            x2 = acc[:, half:]
            o_ref[:, lo:lo + half] = (x1 * cs - x2 * sn).astype(BF16)
            o_ref[:, lo + half:lo + RET_QK_DIM] = (x2 * cs + x1 * sn).astype(BF16)
        else:
            o_ref[:, lo:lo + RET_QK_DIM] = acc.astype(BF16)


def _cast_plan(arr, n_grid):
    cols = arr.shape[-1]
    rows = arr.size // cols
    steps = max(s for s in range(1, n_grid + 1) if rows % s == 0 and (rows // s) % BF16_ROWS == 0)
    return arr.reshape(rows, cols), steps, rows // steps


def _ret_in_proj(x2d, gain, w, cos, sin, seq, side_weights):
    t, d = x2d.shape
    n = w.shape[-1]
    tm = PROJ_TM
    tps = seq // tm
    n_grid = t // tm
    plans = [_cast_plan(arr, n_grid) for arr in side_weights]
    side_specs = [pl.BlockSpec((rps, view.shape[1]), lambda i, last=steps - 1: (jnp.minimum(i, last), 0))
                  for view, steps, rps in plans]
    kern = functools.partial(_ret_in_kernel, n_side=len(plans))
    outs = pl.pallas_call(
        kern,
        grid=(n_grid,),
        in_specs=[
            pl.BlockSpec((tm, d), lambda i: (i, 0)),
            _resident(gain), pl.BlockSpec(memory_space=pl.ANY),
            pl.BlockSpec((tm, RET_QK_DIM // 2), lambda i: (i % tps, 0)),
            pl.BlockSpec((tm, RET_QK_DIM // 2), lambda i: (i % tps, 0)),
        ] + side_specs,
        out_specs=[pl.BlockSpec((tm, n), lambda i: (i, 0))] + side_specs,
        out_shape=[jax.ShapeDtypeStruct((t, n), BF16)]
        + [jax.ShapeDtypeStruct(view.shape, BF16) for view, _, _ in plans],
        scratch_shapes=[pltpu.VMEM((tm, d), BF16)] + _weight_scratch(d, n),
        compiler_params=_params("arbitrary"),
        name="ret_in_proj",
    )(x2d, gain, w, cos, sin, *[view for view, _, _ in plans])
    return outs[0], [o.reshape(arr.shape) for o, arr in zip(outs[1:], side_weights)]


def _ret_core_kernel(q_ref, k_ref, v_ref, g_ref, dmat_ref, qd_ref, kd_ref, gn_ref, o_ref, state_ref, *, n_steps):
    dk, dv = RET_QK_DIM, RET_V_DIM
    state_ref[...] = jnp.zeros_like(state_ref)

    for sc in range(n_steps):
        rows = slice(sc * RET_SUPER, (sc + 1) * RET_SUPER)
        for hh in range(RET_HEADS_PER_STEP):
            qd = qd_ref[hh]
            kd = kd_ref[hh]
            step_decay = qd[RET_SUPER - 1:RET_SUPER, 0:1]
            q = q_ref[0, rows, hh * dk:(hh + 1) * dk]
            k = k_ref[0, rows, hh * dk:(hh + 1) * dk]
            v = v_ref[0, rows, hh * dv:(hh + 1) * dv]
            scores = _dot_nt(q, k) * dmat_ref[hh]
            inner = _dot(scores.astype(BF16), v)
            state = state_ref[hh]
            q_scaled = (q.astype(F32) * qd).astype(BF16)
            cross = _dot(q_scaled, state.astype(BF16))
            k_scaled = (k.astype(F32) * kd).astype(BF16)
            state_ref[hh] = state * step_decay + _dot_tn(k_scaled, v)
            out = inner + cross
            out = out * _rms_scale(out) * gn_ref[hh]
            g = g_ref[0, rows, hh * dv:(hh + 1) * dv]
            o_ref[0, rows, hh * dv:(hh + 1) * dv] = out.astype(BF16) * _silu(g)


def _ret_core(qkvg, dmat, qd, kd, gn, batch, seq):
    hd, dk, dv = RET_HEADS, RET_QK_DIM, RET_V_DIM
    hps = RET_HEADS_PER_STEP
    groups = hd // hps
    qkvg3 = qkvg.reshape(batch, seq, qkvg.shape[-1])
    k_blk0 = groups
    v_blk0 = 2 * hd * dk // (hps * dv)
    g_blk0 = v_blk0 + groups
    kern = functools.partial(_ret_core_kernel, n_steps=seq // RET_SUPER)
    return pl.pallas_call(
        kern,
        grid=(batch, groups),
        in_specs=[
            pl.BlockSpec((1, seq, hps * dk), lambda b, h: (b, 0, h)),
            pl.BlockSpec((1, seq, hps * dk), lambda b, h: (b, 0, k_blk0 + h)),
            pl.BlockSpec((1, seq, hps * dv), lambda b, h: (b, 0, v_blk0 + h)),
            pl.BlockSpec((1, seq, hps * dv), lambda b, h: (b, 0, g_blk0 + h)),
            pl.BlockSpec((hps, RET_SUPER, RET_SUPER), lambda b, h: (h, 0, 0)),
            pl.BlockSpec((hps, RET_SUPER, dk), lambda b, h: (h, 0, 0)),
            pl.BlockSpec((hps, RET_SUPER, dk), lambda b, h: (h, 0, 0)),
            pl.BlockSpec((hps, 1, dv), lambda b, h: (h, 0, 0)),
        ],
        out_specs=pl.BlockSpec((1, seq, hps * dv), lambda b, h: (b, 0, h)),
        out_shape=jax.ShapeDtypeStruct((batch, seq, hd * dv), BF16),
        scratch_shapes=[pltpu.VMEM((hps, dk, dv), F32)],
        compiler_params=_params("parallel", "parallel"),
        name="ret_core",
    )(qkvg3, qkvg3, qkvg3, qkvg3, dmat, qd, kd, gn)


def _mix_ffn_kernel(a_ref, wo_ref, res_ref, gain_ref, w_in_ref, cw_ref, cb_ref, w_out_ref, o_ref,
                    h_ref, act_ref, carry_ref, *, tiles_per_seq):
    i = pl.program_id(0)
    tm = res_ref.shape[0]
    ffn_dim = w_out_ref.shape[0]

    x1 = res_ref[...] + _dot(a_ref[...], wo_ref[...])
    o_ref[...] = x1
    h_ref[...] = (x1 * _rms_scale(x1) * gain_ref[...]).astype(BF16)

    seq_start = lax.rem(i, tiles_per_seq) == 0
    row = lax.broadcasted_iota(jnp.int32, (tm, 1), 0)
    for c in range(ffn_dim // FFN_CHUNK):
        lo = c * FFN_CHUNK
        a = _dot(h_ref[...], w_in_ref[:, lo:lo + FFN_CHUNK])
        g = _dot(h_ref[...], w_in_ref[:, ffn_dim + lo:ffn_dim + lo + FFN_CHUNK])
        prev = jnp.where(seq_start, 0.0, carry_ref[:, lo:lo + FFN_CHUNK])
        prev1 = prev[SUBLANES - 1:SUBLANES, :]
        prev2 = prev[SUBLANES - 2:SUBLANES - 1, :]
        g1 = jnp.where(row == 0, prev1, pltpu.roll(g, 1, 0))
        g2 = jnp.where(row == 0, prev2, jnp.where(row == 1, prev1, pltpu.roll(g, 2, 0)))
        carry_ref[:, lo:lo + FFN_CHUNK] = g[tm - SUBLANES:tm, :]
        cw = cw_ref[:, lo:lo + FFN_CHUNK]
        gc = g2 * cw[0:1, :] + g1 * cw[1:2, :] + g * cw[2:3, :] + cb_ref[:, lo:lo + FFN_CHUNK]
        act_ref[:, lo:lo + FFN_CHUNK] = (_silu(gc) * a).astype(BF16)

    o_ref[...] += _dot(act_ref[...], w_out_ref[...])


def _resident_layer(arr, layer):
    return pl.BlockSpec((None,) + arr.shape[1:], lambda i: (layer,) + (0,) * (arr.ndim - 1),
                        pipeline_mode=pl.Buffered(1))


def _mix_ffn(a, w_o, res, gain, w_in, conv_w, conv_b, w_out, seq, layer):
    t, d = res.shape
    ka = a.shape[1]
    f = w_out.shape[1]
    tm = FFN_TM
    kern = functools.partial(_mix_ffn_kernel, tiles_per_seq=seq // tm)
    return pl.pallas_call(
        kern,
        grid=(t // tm,),
        in_specs=[
            pl.BlockSpec((tm, ka), lambda i: (i, 0)),
            _resident_layer(w_o, 0),
            pl.BlockSpec((tm, d), lambda i: (i, 0)),
            _resident(gain), _resident_layer(w_in, layer), _resident(conv_w), _resident(conv_b),
            _resident_layer(w_out, layer),
        ],
        out_specs=pl.BlockSpec((tm, d), lambda i: (i, 0)),
        out_shape=jax.ShapeDtypeStruct((t, d), F32),
        scratch_shapes=[pltpu.VMEM((tm, d), BF16), pltpu.VMEM((tm, f), BF16), pltpu.VMEM((SUBLANES, f), F32)],
        compiler_params=_params("arbitrary"),
        name="mix_ffn",
    )(a, w_o, res, gain, w_in, conv_w, conv_b, w_out)


def _mla_proj_kernel(x_ref, gain_ref, w_in_ref, qn_ref, kvn_ref, wq_ref, wkv_ref,
                     c_ref, s_ref, gq_ref, gk_ref, q_ref, k_ref, v_ref):
    x = x_ref[...]
    h = (x * _rms_scale(x) * gain_ref[...]).astype(BF16)
    p = _dot(h, w_in_ref[...])
    c_q = p[:, :MLA_Q_RANK]
    c_kv = p[:, MLA_Q_RANK:MLA_Q_RANK + MLA_KV_RANK]
    k_r = p[:, MLA_Q_RANK + MLA_KV_RANK:MLA_Q_RANK + MLA_KV_RANK + LANES]
    k_s = p[:, MLA_Q_RANK + MLA_KV_RANK + LANES:]
    c_q = (c_q * _rms_scale(c_q) * qn_ref[...]).astype(BF16)
    c_kv = (c_kv * _rms_scale(c_kv) * kvn_ref[...]).astype(BF16)

    cos = c_ref[...]
    sin = s_ref[...]
    gq_nope, gq_rope, gq_swap = gq_ref[0:1, :], gq_ref[1:2, :], gq_ref[2:3, :]
    gk_nope, gk_rope, gk_swap = gk_ref[0:1, :], gk_ref[1:2, :], gk_ref[2:3, :]
    inv_dim = 1.0 / MLA_QK_DIM
    q_scale = MLA_QK_DIM ** -0.5 * LOG2_E

    k_rope_sq = k_r * k_r
    k_rope_rot = k_r * gk_rope * cos + k_s * gk_swap * sin

    def project(pair):
        q2 = _dot(c_q, wq_ref[:, pair * 2 * MLA_Q_COLS:(pair + 1) * 2 * MLA_Q_COLS])
        kv2 = _dot(c_kv, wkv_ref[:, pair * 2 * MLA_HEAD_PAD:(pair + 1) * 2 * MLA_HEAD_PAD])
        return q2, kv2

    def finish(pair, q2, kv2):
        for sub in range(2):
            hh = 2 * pair + sub
            q_n = q2[:, sub * MLA_Q_COLS:sub * MLA_Q_COLS + LANES]
            q_r = q2[:, sub * MLA_Q_COLS + LANES:sub * MLA_Q_COLS + 2 * LANES]
            q_s = q2[:, sub * MLA_Q_COLS + 2 * LANES:(sub + 1) * MLA_Q_COLS]
            ss = jnp.sum(q_n * q_n + q_r * q_r, axis=-1, keepdims=True)
            r = lax.rsqrt(ss * inv_dim + RMS_EPS) * q_scale
            q_ref[0, hh, :, 0:LANES] = (q_n * r * gq_nope).astype(BF16)
            q_ref[0, hh, :, LANES:2 * LANES] = ((q_r * gq_rope * cos + q_s * gq_swap * sin) * r).astype(BF16)

            k_n = kv2[:, sub * MLA_HEAD_PAD:sub * MLA_HEAD_PAD + LANES]
            ssk = jnp.sum(k_n * k_n + k_rope_sq, axis=-1, keepdims=True)
            rk = lax.rsqrt(ssk * inv_dim + RMS_EPS)
            k_ref[0, hh, :, 0:LANES] = (k_n * rk * gk_nope).astype(BF16)
            k_ref[0, hh, :, LANES:2 * LANES] = (k_rope_rot * rk).astype(BF16)
            v_ref[0, hh] = kv2[:, sub * MLA_HEAD_PAD + LANES:(sub + 1) * MLA_HEAD_PAD].astype(BF16)

    n_pairs = MLA_HEADS // 2
    pending = project(0)
    for pair in range(n_pairs):
        upcoming = project(pair + 1) if pair + 1 < n_pairs else None
        finish(pair, *pending)
        pending = upcoming


def _mla_proj(x2d, gain, w_in, qn, kvn, wq, wkv, cos, sin, gq, gk, batch, seq):
    t, d = x2d.shape
    tm = MLA_TM
    tps = seq // tm
    hd = MLA_HEADS

    return pl.pallas_call(
        _mla_proj_kernel,
        grid=(t // tm,),
        in_specs=[
            pl.BlockSpec((tm, d), lambda i: (i, 0)),
            _resident(gain), _resident(w_in), _resident(qn), _resident(kvn), _resident(wq), _resident(wkv),
            pl.BlockSpec((tm, LANES), lambda i: (i % tps, 0)),
            pl.BlockSpec((tm, LANES), lambda i: (i % tps, 0)),
            _resident(gq), _resident(gk),
        ],
        out_specs=[
            pl.BlockSpec((1, hd, tm, MLA_HEAD_PAD), lambda i: (i // tps, 0, i % tps, 0)),
            pl.BlockSpec((1, hd, tm, MLA_HEAD_PAD), lambda i: (i // tps, 0, i % tps, 0)),
            pl.BlockSpec((1, hd, tm, MLA_V_DIM), lambda i: (i // tps, 0, i % tps, 0)),
        ],
        out_shape=[
            jax.ShapeDtypeStruct((batch, hd, seq, MLA_HEAD_PAD), BF16),
            jax.ShapeDtypeStruct((batch, hd, seq, MLA_HEAD_PAD), BF16),
            jax.ShapeDtypeStruct((batch, hd, seq, MLA_V_DIM), BF16),
        ],
        compiler_params=_params("parallel"),
        name="mla_proj",
    )(x2d, gain, w_in, qn, kvn, wq, wkv, cos, sin, gq, gk)


def _attn_kernel(q_ref, k_ref, v_ref, o_ref, vext_ref, *, n_q_blocks):
    tq = ATT_TQ
    row = lax.broadcasted_iota(jnp.int32, (tq, tq), 0) // CHUNK
    col = lax.broadcasted_iota(jnp.int32, (tq, tq), 1) // CHUNK
    visible = col <= row

    vext_ref[:, :, :MLA_V_DIM] = v_ref[0]
    vext_ref[:, :, MLA_V_DIM:] = jnp.ones(vext_ref.shape[:2] + (MLA_V_DIM,), BF16)

    def scores(hh, qi):
        q0 = qi * tq
        q = q_ref[0, hh, q0:q0 + tq, :]
        s_diag = jnp.where(visible, _dot_nt(q, k_ref[0, hh, q0:q0 + tq, :]), MASK_VALUE)
        s_off = _dot_nt(q, k_ref[0, hh, 0:q0, :]) if qi > 0 else None
        return s_diag, s_off

    def finish(hh, qi, s_diag, s_off):
        q0 = qi * tq
        m = jnp.max(s_diag, axis=-1, keepdims=True)
        if qi > 0:
            m = jnp.maximum(m, jnp.max(s_off, axis=-1, keepdims=True))
        p_diag = jnp.exp2(s_diag - m)
        acc = _dot(p_diag.astype(BF16), vext_ref[hh, q0:q0 + tq, :])
        if qi > 0:
            p_off = jnp.exp2(s_off - m)
            acc = acc + _dot(p_off.astype(BF16), vext_ref[hh, 0:q0, :])
        out = acc[:, :MLA_V_DIM] / acc[:, MLA_V_DIM:]
        o_ref[0, q0:q0 + tq, hh * MLA_V_DIM:(hh + 1) * MLA_V_DIM] = out.astype(BF16)

    work = [(hh, qi) for hh in range(ATT_HEADS_PER_STEP) for qi in range(n_q_blocks - 1, -1, -1)]
    queue = [scores(*item) for item in work[:ATT_AHEAD]]
    for pos, item in enumerate(work):
        if pos + ATT_AHEAD < len(work):
            queue.append(scores(*work[pos + ATT_AHEAD]))
        finish(*item, *queue.pop(0))


def _attention(q, k, v):
    batch, hd, seq, _ = q.shape
    hps = ATT_HEADS_PER_STEP
    kern = functools.partial(_attn_kernel, n_q_blocks=seq // ATT_TQ)
    return pl.pallas_call(
        kern,
        grid=(batch, hd // hps),
        in_specs=[
            pl.BlockSpec((1, hps, seq, MLA_HEAD_PAD), lambda b, h: (b, h, 0, 0)),
            pl.BlockSpec((1, hps, seq, MLA_HEAD_PAD), lambda b, h: (b, h, 0, 0)),
            pl.BlockSpec((1, hps, seq, MLA_V_DIM), lambda b, h: (b, h, 0, 0)),
        ],
        out_specs=pl.BlockSpec((1, seq, hps * MLA_V_DIM), lambda b, h: (b, 0, h)),
        out_shape=jax.ShapeDtypeStruct((batch, seq, hd * MLA_V_DIM), BF16),
        scratch_shapes=[pltpu.VMEM((hps, seq, 2 * MLA_V_DIM), BF16)],
        compiler_params=_params("parallel", "parallel"),
        name="mla_attention",
    )(q, k, v)


def _rope_tables(seq, half):
    inv_freq = ROPE_THETA ** (-jnp.arange(half, dtype=F32) / half)
    ang = jnp.arange(seq).astype(F32)[:, None] * inv_freq[None, :]
    return jnp.cos(ang), jnp.sin(ang)


def _retention_decay_tables():
    sc = RET_SUPER
    log_gamma = jnp.log1p(-jnp.exp2(RET_GAMMA_BASE - jnp.arange(RET_HEADS, dtype=F32)))
    idx = jnp.arange(sc, dtype=F32)
    dist = jnp.abs(idx[:, None] - idx[None, :])
    chunk = jnp.arange(sc) // CHUNK
    visible = chunk[None, :] <= chunk[:, None]
    dmat = jnp.where(visible[None], jnp.exp(log_gamma[:, None, None] * dist[None]), 0.0)
    lane_dense = (RET_HEADS, sc, RET_QK_DIM)
    qd = jnp.broadcast_to(jnp.exp(log_gamma[:, None] * (idx + 1.0))[:, :, None], lane_dense)
    kd = jnp.broadcast_to(jnp.exp(log_gamma[:, None] * (sc - 1.0 - idx))[:, :, None], lane_dense)
    return dmat, qd, kd


def _swap_halves(w):
    half = w.shape[-1] // 2
    return jnp.concatenate([w[..., half:], w[..., :half]], axis=-1)


def _pad_lanes(w):
    return jnp.pad(w, [(0, 0)] * (w.ndim - 1) + [(0, LANES - w.shape[-1])])


def _mla_weights(w_in, w_qb, q_head_g, k_head_g):
    rq, rkv = MLA_Q_RANK, MLA_KV_RANK
    w_kr = w_in[:, rq + rkv:]
    w_in_l = jnp.concatenate([w_in[:, :rq + rkv], _pad_lanes(w_kr), _pad_lanes(_swap_halves(w_kr))], axis=1)
    w_qb3 = w_qb.reshape(rq, MLA_HEADS, MLA_QK_DIM)
    nope, rope = w_qb3[..., :MLA_NOPE_DIM], w_qb3[..., MLA_NOPE_DIM:]
    wq = jnp.concatenate([nope, _pad_lanes(rope), _pad_lanes(_swap_halves(rope))], axis=-1)
    wq = wq.reshape(rq, MLA_HEADS * MLA_Q_COLS)

    def gains(g):
        rope_g = g[MLA_NOPE_DIM:]
        return jnp.stack([g[:MLA_NOPE_DIM], _pad_lanes(rope_g), _pad_lanes(_swap_halves(rope_g))])

    return w_in_l, wq, gains(q_head_g), gains(k_head_g)


def kernel(x, ret_norm, ret_w_in, ret_gn, ret_w_out, mla_norm, mla_w_in, mla_q_norm, mla_w_qb, mla_kv_norm,
           mla_w_kvb, mla_q_head_norm, mla_k_head_norm, mla_w_out, ffn_norm, ffn_w_in, ffn_conv_w, ffn_conv_b,
           ffn_w_out):
    batch, seq, d = x.shape
    x2d = x.reshape(batch * seq, d)

    cos_r, sin_r = _rope_tables(seq, RET_QK_DIM // 2)
    dmat, qd, kd = _retention_decay_tables()
    w_in_l, wq, gq, gk = _mla_weights(mla_w_in[0], mla_w_qb[0], mla_q_head_norm[0], mla_k_head_norm[0])
    side = [ret_w_out, ffn_w_in, ffn_w_out, w_in_l, wq, mla_w_kvb[0], mla_w_out]
    qkvg, (ret_w_out_b, ffn_w_in_b, ffn_w_out_b, w_in_l_b, wq_b, wkv_b, mla_w_out_b) = _ret_in_proj(
        x2d, ret_norm[0][None, :], ret_w_in, cos_r, sin_r, seq, side)
    ret = _ret_core(qkvg, dmat, qd, kd, ret_gn[0][:, None, :], batch, seq)
    x2d = _mix_ffn(ret.reshape(batch * seq, -1), ret_w_out_b, x2d, ffn_norm[0][None, :],
                   ffn_w_in_b, ffn_conv_w[0], ffn_conv_b[0][None, :], ffn_w_out_b, seq, layer=0)

    cos_m, sin_m = _rope_tables(seq, MLA_ROPE_DIM // 2)
    pad = jnp.zeros((seq, LANES - MLA_ROPE_DIM), F32)
    cos_t = jnp.concatenate([cos_m, cos_m, pad], axis=1)
    sin_t = jnp.concatenate([-sin_m, sin_m, pad], axis=1)
    q, k, v = _mla_proj(x2d, mla_norm[0][None, :], w_in_l_b, mla_q_norm[0][None, :],
                        mla_kv_norm[0][None, :], wq_b, wkv_b, cos_t, sin_t, gq, gk, batch, seq)
    att = _attention(q, k, v)
    x2d = _mix_ffn(att.reshape(batch * seq, -1), mla_w_out_b, x2d, ffn_norm[1][None, :],
                   ffn_w_in_b, ffn_conv_w[1], ffn_conv_b[1][None, :], ffn_w_out_b, seq, layer=1)
    return x2d.reshape(batch, seq, d)
```

```python
import functools

import jax
import jax.numpy as jnp
from jax import lax
from jax.experimental import pallas as pl
from jax.experimental.pallas import tpu as pltpu

F32 = jnp.float32
BF16 = jnp.bfloat16

CHUNK = 64
RMS_EPS = 1e-6
ROPE_THETA = 10000.0
RET_HEADS = 4
RET_QK_DIM = 256
RET_V_DIM = 512
RET_GAMMA_BASE = -5.0
MLA_HEADS = 8
MLA_Q_RANK = 384
MLA_KV_RANK = 256
MLA_NOPE_DIM = 128
MLA_ROPE_DIM = 64
MLA_V_DIM = 128
MLA_QK_DIM = MLA_NOPE_DIM + MLA_ROPE_DIM
MLA_HEAD_PAD = 256
MLA_Q_COLS = 3 * 128
MASK_VALUE = -1e30
CONV_WIDTH = 3
LOG2_E = 1.4426950408889634

LANES = 128
SUBLANES = 8
VMEM_LIMIT_BYTES = 58 * 1024 * 1024
BF16_ROWS = 16

RET_SUPER = 256
PROJ_TM = 512
RET_HEADS_PER_STEP = 2
FFN_TM = 1024
FFN_CHUNK = 256
MLA_TM = 1024
W_STAGE_BYTES = 384 * 1024
W_STAGE_SLOTS = 4
ATT_TQ = 256
ATT_HEADS_PER_STEP = 2
ATT_AHEAD = 2


def _params(*sem):
    return pltpu.CompilerParams(dimension_semantics=sem, vmem_limit_bytes=VMEM_LIMIT_BYTES)


def _resident(arr):
    return pl.BlockSpec(arr.shape, lambda i: (0,) * arr.ndim, pipeline_mode=pl.Buffered(1))


def _stage_rows(rows, cols):
    limit = W_STAGE_BYTES // (cols * 4)
    fits = [r for r in range(BF16_ROWS, limit + 1, BF16_ROWS) if rows % r == 0]
    return max(fits)


def _weight_scratch(rows, cols):
    return [pltpu.VMEM((rows, cols), BF16), pltpu.VMEM((W_STAGE_SLOTS, _stage_rows(rows, cols), cols), F32),
            pltpu.SemaphoreType.DMA((W_STAGE_SLOTS,))]


def _load_weight(src_ref, dst_ref, stage_ref, sem_ref, col_starts=None):
    slots, rows = stage_ref.shape[0], stage_ref.shape[1]
    n_chunks = dst_ref.shape[0] // rows
    lookahead = slots - 1
    assert n_chunks >= lookahead

    def copy(k, slot):
        return pltpu.make_async_copy(src_ref.at[pl.ds(k * rows, rows), :], stage_ref.at[slot], sem_ref.at[slot])

    for k in range(lookahead):
        copy(k, k).start()

    def body(k, carry):
        nxt = k + lookahead

        @pl.when(nxt < n_chunks)
        def _():
            copy(nxt, lax.rem(nxt, slots)).start()

        slot = lax.rem(k, slots)
        copy(k, slot).wait()
        r0 = pl.multiple_of(k * rows, rows)
        if col_starts is None:
            dst_ref[pl.ds(r0, rows), :] = stage_ref[slot].astype(BF16)
        else:
            for piece, lo in enumerate(col_starts):
                dst_ref[pl.ds(r0, rows), piece * RET_QK_DIM:(piece + 1) * RET_QK_DIM] = (
                    stage_ref[slot, :, lo:lo + RET_QK_DIM].astype(BF16))
        return carry

    lax.fori_loop(0, n_chunks, body, 0)


def _rms_scale(x):
    return lax.rsqrt(jnp.mean(x * x, axis=-1, keepdims=True) + RMS_EPS)


def _silu(x):
    half = 0.5 * x
    return half + half * jnp.tanh(half)


def _dot(a, b):
    return jnp.dot(a, b, preferred_element_type=F32)


def _dot_nt(a, b):
    return lax.dot_general(a, b, (((1,), (1,)), ((), ())), preferred_element_type=F32)


def _dot_tn(a, b):
    return lax.dot_general(a, b, (((0,), (0,)), ((), ())), preferred_element_type=F32)


def _ret_col_starts():
    hd, dk, dv, hps = RET_HEADS, RET_QK_DIM, RET_V_DIM, RET_HEADS_PER_STEP
    q0, k0, v0, g0 = 0, hd * dk, 2 * hd * dk, 2 * hd * dk + hd * dv
    starts = []
    for group in range(hd // hps):
        heads = range(group * hps, (group + 1) * hps)
        starts += [q0 + h * dk for h in heads]
        starts += [k0 + h * dk for h in heads]
        starts += [v0 + h * dv + part for h in heads for part in range(0, dv, dk)]
        starts += [g0 + h * dv + part for h in heads for part in range(0, dv, dk)]
    return starts


def _ret_in_kernel(*refs, n_side):
    x_ref, gain_ref, w_hbm, cos_ref, sin_ref = refs[:5]
    side_in = refs[5:5 + n_side]
    o_ref = refs[5 + n_side]
    side_out = refs[6 + n_side:6 + 2 * n_side]
    h_ref, w_ref, stage_ref, sem_ref = refs[6 + 2 * n_side:]
    i = pl.program_id(0)
    half = RET_QK_DIM // 2

    @pl.when(i == 0)
    def _():
        _load_weight(w_hbm.at[0], w_ref, stage_ref, sem_ref, col_starts=_ret_col_starts())

    for src_ref, dst_ref in zip(side_in, side_out):
        dst_ref[...] = src_ref[...].astype(BF16)

    x = x_ref[...]
    h_ref[...] = (x * _rms_scale(x) * gain_ref[...]).astype(BF16)
    cos = cos_ref[...]
    sin = sin_ref[...]
    k_scale = RET_QK_DIM ** -0.5
    cos_k = cos * k_scale
    sin_k = sin * k_scale
    hps = RET_HEADS_PER_STEP
    per_group = w_ref.shape[1] // RET_QK_DIM // (RET_HEADS // hps)
    for c in range(w_ref.shape[1] // RET_QK_DIM):
        lo = c * RET_QK_DIM
        acc = _dot(h_ref[...], w_ref[:, lo:lo + RET_QK_DIM])
        if c % per_group < 2 * hps:
            cs, sn = (cos, sin) if c % per_group < hps else (cos_k, sin_k)
            x1 = acc[:, :half]
            x2 = acc[:, half:]
            o_ref[:, lo:lo + half] = (x1 * cs - x2 * sn).astype(BF16)
            o_ref[:, lo + half:lo + RET_QK_DIM] = (x2 * cs + x1 * sn).astype(BF16)
        else:
            o_ref[:, lo:lo + RET_QK_DIM] = acc.astype(BF16)


def _cast_plan(arr, n_grid):
    cols = arr.shape[-1]
    rows = arr.size // cols
    steps = max(s for s in range(1, n_grid + 1) if rows % s == 0 and (rows // s) % BF16_ROWS == 0)
    return arr.reshape(rows, cols), steps, rows // steps


def _ret_in_proj(x2d, gain, w, cos, sin, seq, side_weights):
    t, d = x2d.shape
    n = w.shape[-1]
    tm = PROJ_TM
    tps = seq // tm
    n_grid = t // tm
    plans = [_cast_plan(arr, n_grid) for arr in side_weights]
    side_specs = [pl.BlockSpec((rps, view.shape[1]), lambda i, last=steps - 1: (jnp.minimum(i, last), 0))
                  for view, steps, rps in plans]
    kern = functools.partial(_ret_in_kernel, n_side=len(plans))
    outs = pl.pallas_call(
        kern,
        grid=(n_grid,),
        in_specs=[
            pl.BlockSpec((tm, d), lambda i: (i, 0)),
            _resident(gain), pl.BlockSpec(memory_space=pl.ANY),
            pl.BlockSpec((tm, RET_QK_DIM // 2), lambda i: (i % tps, 0)),
            pl.BlockSpec((tm, RET_QK_DIM // 2), lambda i: (i % tps, 0)),
        ] + side_specs,
        out_specs=[pl.BlockSpec((tm, n), lambda i: (i, 0))] + side_specs,
        out_shape=[jax.ShapeDtypeStruct((t, n), BF16)]
        + [jax.ShapeDtypeStruct(view.shape, BF16) for view, _, _ in plans],
        scratch_shapes=[pltpu.VMEM((tm, d), BF16)] + _weight_scratch(d, n),
        compiler_params=_params("arbitrary"),
        name="ret_in_proj",
    )(x2d, gain, w, cos, sin, *[view for view, _, _ in plans])
    return outs[0], [o.reshape(arr.shape) for o, arr in zip(outs[1:], side_weights)]


def _ret_core_kernel(x_ref, dmat_ref, qd_ref, kd_ref, gn_ref, o_ref, state_ref, *, n_steps):
    dk, dv, hps = RET_QK_DIM, RET_V_DIM, RET_HEADS_PER_STEP
    k_lo, v_lo, g_lo = hps * dk, 2 * hps * dk, 2 * hps * dk + hps * dv
    state_ref[...] = jnp.zeros_like(state_ref)

    for sc in range(n_steps):
        rows = slice(sc * RET_SUPER, (sc + 1) * RET_SUPER)
        for hh in range(RET_HEADS_PER_STEP):
            qd = qd_ref[hh]
            kd = kd_ref[hh]
            step_decay = qd[RET_SUPER - 1:RET_SUPER, :]
            q = x_ref[0, rows, hh * dk:(hh + 1) * dk]
            k = x_ref[0, rows, k_lo + hh * dk:k_lo + (hh + 1) * dk]
            v = x_ref[0, rows, v_lo + hh * dv:v_lo + (hh + 1) * dv]
            scores = _dot_nt(q, k) * dmat_ref[hh]
            inner = _dot(scores.astype(BF16), v)
            state = state_ref[hh]
            q_scaled = (q.astype(F32) * qd).astype(BF16)
            cross = _dot(q_scaled, state.astype(BF16))
            k_scaled = (k.astype(F32) * kd).astype(BF16)
            state_ref[hh] = state * step_decay + _dot_tn(k_scaled, v)
            out = inner + cross
            out = out * _rms_scale(out) * gn_ref[hh]
            g = x_ref[0, rows, g_lo + hh * dv:g_lo + (hh + 1) * dv]
            o_ref[0, rows, hh * dv:(hh + 1) * dv] = out.astype(BF16) * _silu(g)


def _ret_core(qkvg, dmat, qd, kd, gn, batch, seq):
    hd, dk, dv = RET_HEADS, RET_QK_DIM, RET_V_DIM
    hps = RET_HEADS_PER_STEP
    groups = hd // hps
    qkvg3 = qkvg.reshape(batch, seq, qkvg.shape[-1])
    group_cols = qkvg.shape[-1] // groups
    kern = functools.partial(_ret_core_kernel, n_steps=seq // RET_SUPER)
    return pl.pallas_call(
        kern,
        grid=(batch, groups),
        in_specs=[
            pl.BlockSpec((1, seq, group_cols), lambda b, h: (b, 0, h)),
            pl.BlockSpec((hps, RET_SUPER, RET_SUPER), lambda b, h: (h, 0, 0)),
            pl.BlockSpec((hps, RET_SUPER, 1), lambda b, h: (h, 0, 0)),
            pl.BlockSpec((hps, RET_SUPER, 1), lambda b, h: (h, 0, 0)),
            pl.BlockSpec((hps, 1, dv), lambda b, h: (h, 0, 0)),
        ],
        out_specs=pl.BlockSpec((1, seq, hps * dv), lambda b, h: (b, 0, h)),
        out_shape=jax.ShapeDtypeStruct((batch, seq, hd * dv), BF16),
        scratch_shapes=[pltpu.VMEM((hps, dk, dv), F32)],
        compiler_params=_params("parallel", "parallel"),
        name="ret_core",
    )(qkvg3, dmat, qd, kd, gn)


def _mix_ffn_kernel(a_ref, wo_ref, res_ref, gain_ref, w_in_ref, cw_ref, cb_ref, w_out_ref, o_ref,
                    h_ref, act_ref, carry_ref, *, tiles_per_seq):
    i = pl.program_id(0)
    tm = res_ref.shape[0]
    ffn_dim = w_out_ref.shape[0]

    x1 = res_ref[...] + _dot(a_ref[...], wo_ref[...])
    o_ref[...] = x1
    h_ref[...] = (x1 * _rms_scale(x1) * gain_ref[...]).astype(BF16)

    seq_start = lax.rem(i, tiles_per_seq) == 0
    row = lax.broadcasted_iota(jnp.int32, (tm, 1), 0)
    for c in range(ffn_dim // FFN_CHUNK):
        lo = c * FFN_CHUNK
        a = _dot(h_ref[...], w_in_ref[:, lo:lo + FFN_CHUNK])
        g = _dot(h_ref[...], w_in_ref[:, ffn_dim + lo:ffn_dim + lo + FFN_CHUNK])
        prev = jnp.where(seq_start, 0.0, carry_ref[:, lo:lo + FFN_CHUNK])
        prev1 = prev[SUBLANES - 1:SUBLANES, :]
        prev2 = prev[SUBLANES - 2:SUBLANES - 1, :]
        g1 = jnp.where(row == 0, prev1, pltpu.roll(g, 1, 0))
        g2 = jnp.where(row == 0, prev2, jnp.where(row == 1, prev1, pltpu.roll(g, 2, 0)))
        carry_ref[:, lo:lo + FFN_CHUNK] = g[tm - SUBLANES:tm, :]
        cw = cw_ref[:, lo:lo + FFN_CHUNK]
        gc = g2 * cw[0:1, :] + g1 * cw[1:2, :] + g * cw[2:3, :] + cb_ref[:, lo:lo + FFN_CHUNK]
        act_ref[:, lo:lo + FFN_CHUNK] = (_silu(gc) * a).astype(BF16)

    o_ref[...] += _dot(act_ref[...], w_out_ref[...])


def _resident_layer(arr, layer):
    return pl.BlockSpec((None,) + arr.shape[1:], lambda i: (layer,) + (0,) * (arr.ndim - 1),
                        pipeline_mode=pl.Buffered(1))


def _mix_ffn(a, w_o, res, gain, w_in, conv_w, conv_b, w_out, seq, layer):
    t, d = res.shape
    ka = a.shape[1]
    f = w_out.shape[1]
    tm = FFN_TM
    kern = functools.partial(_mix_ffn_kernel, tiles_per_seq=seq // tm)
    return pl.pallas_call(
        kern,
        grid=(t // tm,),
        in_specs=[
            pl.BlockSpec((tm, ka), lambda i: (i, 0)),
            _resident_layer(w_o, 0),
            pl.BlockSpec((tm, d), lambda i: (i, 0)),
            _resident(gain), _resident_layer(w_in, layer), _resident(conv_w), _resident(conv_b),
            _resident_layer(w_out, layer),
        ],
        out_specs=pl.BlockSpec((tm, d), lambda i: (i, 0)),
        out_shape=jax.ShapeDtypeStruct((t, d), F32),
        scratch_shapes=[pltpu.VMEM((tm, d), BF16), pltpu.VMEM((tm, f), BF16), pltpu.VMEM((SUBLANES, f), F32)],
        compiler_params=_params("arbitrary"),
        name="mix_ffn",
    )(a, w_o, res, gain, w_in, conv_w, conv_b, w_out)


def _mla_proj_kernel(x_ref, gain_ref, w_in_ref, qn_ref, kvn_ref, wq_ref, wkv_ref,
                     c_ref, s_ref, gq_ref, gk_ref, q_ref, k_ref, v_ref):
    x = x_ref[...]
    h = (x * _rms_scale(x) * gain_ref[...]).astype(BF16)
    p = _dot(h, w_in_ref[...])
    c_q = p[:, :MLA_Q_RANK]
    c_kv = p[:, MLA_Q_RANK:MLA_Q_RANK + MLA_KV_RANK]
    k_r = p[:, MLA_Q_RANK + MLA_KV_RANK:MLA_Q_RANK + MLA_KV_RANK + LANES]
    k_s = p[:, MLA_Q_RANK + MLA_KV_RANK + LANES:]
    c_q = (c_q * _rms_scale(c_q) * qn_ref[...]).astype(BF16)
    c_kv = (c_kv * _rms_scale(c_kv) * kvn_ref[...]).astype(BF16)

    cos = c_ref[...]
    sin = s_ref[...]
    gq_nope, gq_rope, gq_swap = gq_ref[0:1, :], gq_ref[1:2, :], gq_ref[2:3, :]
    gk_nope, gk_rope, gk_swap = gk_ref[0:1, :], gk_ref[1:2, :], gk_ref[2:3, :]
    inv_dim = 1.0 / MLA_QK_DIM
    q_scale = MLA_QK_DIM ** -0.5 * LOG2_E

    k_rope_sq = k_r * k_r
    k_rope_rot = k_r * gk_rope * cos + k_s * gk_swap * sin

    def project(pair):
        q2 = _dot(c_q, wq_ref[:, pair * 2 * MLA_Q_COLS:(pair + 1) * 2 * MLA_Q_COLS])
        kv2 = _dot(c_kv, wkv_ref[:, pair * 2 * MLA_HEAD_PAD:(pair + 1) * 2 * MLA_HEAD_PAD])
        return q2, kv2

    def finish(pair, q2, kv2):
        for sub in range(2):
            hh = 2 * pair + sub
            q_n = q2[:, sub * MLA_Q_COLS:sub * MLA_Q_COLS + LANES]
            q_r = q2[:, sub * MLA_Q_COLS + LANES:sub * MLA_Q_COLS + 2 * LANES]
            q_s = q2[:, sub * MLA_Q_COLS + 2 * LANES:(sub + 1) * MLA_Q_COLS]
            ss = jnp.sum(q_n * q_n + q_r * q_r, axis=-1, keepdims=True)
            r = lax.rsqrt(ss * inv_dim + RMS_EPS) * q_scale
            q_ref[0, hh, :, 0:LANES] = (q_n * r * gq_nope).astype(BF16)
            q_ref[0, hh, :, LANES:2 * LANES] = ((q_r * gq_rope * cos + q_s * gq_swap * sin) * r).astype(BF16)

            k_n = kv2[:, sub * MLA_HEAD_PAD:sub * MLA_HEAD_PAD + LANES]
            ssk = jnp.sum(k_n * k_n + k_rope_sq, axis=-1, keepdims=True)
            rk = lax.rsqrt(ssk * inv_dim + RMS_EPS)
            k_ref[0, hh, :, 0:LANES] = (k_n * rk * gk_nope).astype(BF16)
            k_ref[0, hh, :, LANES:2 * LANES] = (k_rope_rot * rk).astype(BF16)
            v_ref[0, hh] = kv2[:, sub * MLA_HEAD_PAD + LANES:(sub + 1) * MLA_HEAD_PAD].astype(BF16)

    n_pairs = MLA_HEADS // 2
    pending = project(0)
    for pair in range(n_pairs):
        upcoming = project(pair + 1) if pair + 1 < n_pairs else None
        finish(pair, *pending)
        pending = upcoming


def _mla_proj(x2d, gain, w_in, qn, kvn, wq, wkv, cos, sin, gq, gk, batch, seq):
    t, d = x2d.shape
    tm = MLA_TM
    tps = seq // tm
    hd = MLA_HEADS

    return pl.pallas_call(
        _mla_proj_kernel,
        grid=(t // tm,),
        in_specs=[
            pl.BlockSpec((tm, d), lambda i: (i, 0)),
            _resident(gain), _resident(w_in), _resident(qn), _resident(kvn), _resident(wq), _resident(wkv),
            pl.BlockSpec((tm, LANES), lambda i: (i % tps, 0)),
            pl.BlockSpec((tm, LANES), lambda i: (i % tps, 0)),
            _resident(gq), _resident(gk),
        ],
        out_specs=[
            pl.BlockSpec((1, hd, tm, MLA_HEAD_PAD), lambda i: (i // tps, 0, i % tps, 0)),
            pl.BlockSpec((1, hd, tm, MLA_HEAD_PAD), lambda i: (i // tps, 0, i % tps, 0)),
            pl.BlockSpec((1, hd, tm, MLA_V_DIM), lambda i: (i // tps, 0, i % tps, 0)),
        ],
        out_shape=[
            jax.ShapeDtypeStruct((batch, hd, seq, MLA_HEAD_PAD), BF16),
            jax.ShapeDtypeStruct((batch, hd, seq, MLA_HEAD_PAD), BF16),
            jax.ShapeDtypeStruct((batch, hd, seq, MLA_V_DIM), BF16),
        ],
        compiler_params=_params("parallel"),
        name="mla_proj",
    )(x2d, gain, w_in, qn, kvn, wq, wkv, cos, sin, gq, gk)


def _attn_kernel(q_ref, k_ref, v_ref, o_ref, vext_ref, *, n_q_blocks):
    tq = ATT_TQ
    row = lax.broadcasted_iota(jnp.int32, (tq, tq), 0) // CHUNK
    col = lax.broadcasted_iota(jnp.int32, (tq, tq), 1) // CHUNK
    visible = col <= row

    vext_ref[:, :, :MLA_V_DIM] = v_ref[0]
    vext_ref[:, :, MLA_V_DIM:] = jnp.ones(vext_ref.shape[:2] + (MLA_V_DIM,), BF16)

    def scores(hh, qi):
        q0 = qi * tq
        q = q_ref[0, hh, q0:q0 + tq, :]
        s_diag = jnp.where(visible, _dot_nt(q, k_ref[0, hh, q0:q0 + tq, :]), MASK_VALUE)
        s_off = _dot_nt(q, k_ref[0, hh, 0:q0, :]) if qi > 0 else None
        return s_diag, s_off

    def finish(hh, qi, s_diag, s_off):
        q0 = qi * tq
        m = jnp.max(s_diag, axis=-1, keepdims=True)
        if qi > 0:
            m = jnp.maximum(m, jnp.max(s_off, axis=-1, keepdims=True))
        p_diag = jnp.exp2(s_diag - m)
        acc = _dot(p_diag.astype(BF16), vext_ref[hh, q0:q0 + tq, :])
        if qi > 0:
            p_off = jnp.exp2(s_off - m)
            acc = acc + _dot(p_off.astype(BF16), vext_ref[hh, 0:q0, :])
        out = acc[:, :MLA_V_DIM] / acc[:, MLA_V_DIM:]
        o_ref[0, q0:q0 + tq, hh * MLA_V_DIM:(hh + 1) * MLA_V_DIM] = out.astype(BF16)

    work = [(hh, qi) for hh in range(ATT_HEADS_PER_STEP) for qi in range(n_q_blocks - 1, -1, -1)]
    queue = [scores(*item) for item in work[:ATT_AHEAD]]
    for pos, item in enumerate(work):
        if pos + ATT_AHEAD < len(work):
            queue.append(scores(*work[pos + ATT_AHEAD]))
        finish(*item, *queue.pop(0))


def _attention(q, k, v):
    batch, hd, seq, _ = q.shape
    hps = ATT_HEADS_PER_STEP
    kern = functools.partial(_attn_kernel, n_q_blocks=seq // ATT_TQ)
    return pl.pallas_call(
        kern,
        grid=(batch, hd // hps),
        in_specs=[
            pl.BlockSpec((1, hps, seq, MLA_HEAD_PAD), lambda b, h: (b, h, 0, 0)),
            pl.BlockSpec((1, hps, seq, MLA_HEAD_PAD), lambda b, h: (b, h, 0, 0)),
            pl.BlockSpec((1, hps, seq, MLA_V_DIM), lambda b, h: (b, h, 0, 0)),
        ],
        out_specs=pl.BlockSpec((1, seq, hps * MLA_V_DIM), lambda b, h: (b, 0, h)),
        out_shape=jax.ShapeDtypeStruct((batch, seq, hd * MLA_V_DIM), BF16),
        scratch_shapes=[pltpu.VMEM((hps, seq, 2 * MLA_V_DIM), BF16)],
        compiler_params=_params("parallel", "parallel"),
        name="mla_attention",
    )(q, k, v)


def _rope_tables(seq, half):
    inv_freq = ROPE_THETA ** (-jnp.arange(half, dtype=F32) / half)
    ang = jnp.arange(seq).astype(F32)[:, None] * inv_freq[None, :]
    return jnp.cos(ang), jnp.sin(ang)


def _retention_decay_tables():
    sc = RET_SUPER
    log_gamma = jnp.log1p(-jnp.exp2(RET_GAMMA_BASE - jnp.arange(RET_HEADS, dtype=F32)))
    idx = jnp.arange(sc, dtype=F32)
    dist = jnp.abs(idx[:, None] - idx[None, :])
    chunk = jnp.arange(sc) // CHUNK
    visible = chunk[None, :] <= chunk[:, None]
    dmat = jnp.where(visible[None], jnp.exp(log_gamma[:, None, None] * dist[None]), 0.0)
    qd = jnp.exp(log_gamma[:, None] * (idx + 1.0))[:, :, None]
    kd = jnp.exp(log_gamma[:, None] * (sc - 1.0 - idx))[:, :, None]
    return dmat, qd, kd


def _swap_halves(w):
    half = w.shape[-1] // 2
    return jnp.concatenate([w[..., half:], w[..., :half]], axis=-1)


def _pad_lanes(w):
    return jnp.pad(w, [(0, 0)] * (w.ndim - 1) + [(0, LANES - w.shape[-1])])


def _mla_weights(w_in, w_qb, q_head_g, k_head_g):
    rq, rkv = MLA_Q_RANK, MLA_KV_RANK
    w_kr = w_in[:, rq + rkv:]
    w_in_l = jnp.concatenate([w_in[:, :rq + rkv], _pad_lanes(w_kr), _pad_lanes(_swap_halves(w_kr))], axis=1)
    w_qb3 = w_qb.reshape(rq, MLA_HEADS, MLA_QK_DIM)
    nope, rope = w_qb3[..., :MLA_NOPE_DIM], w_qb3[..., MLA_NOPE_DIM:]
    wq = jnp.concatenate([nope, _pad_lanes(rope), _pad_lanes(_swap_halves(rope))], axis=-1)
    wq = wq.reshape(rq, MLA_HEADS * MLA_Q_COLS)

    def gains(g):
        rope_g = g[MLA_NOPE_DIM:]
        return jnp.stack([g[:MLA_NOPE_DIM], _pad_lanes(rope_g), _pad_lanes(_swap_halves(rope_g))])

    return w_in_l, wq, gains(q_head_g), gains(k_head_g)


def kernel(x, ret_norm, ret_w_in, ret_gn, ret_w_out, mla_norm, mla_w_in, mla_q_norm, mla_w_qb, mla_kv_norm,
           mla_w_kvb, mla_q_head_norm, mla_k_head_norm, mla_w_out, ffn_norm, ffn_w_in, ffn_conv_w, ffn_conv_b,
           ffn_w_out):
    batch, seq, d = x.shape
    x2d = x.reshape(batch * seq, d)

    cos_r, sin_r = _rope_tables(seq, RET_QK_DIM // 2)
    dmat, qd, kd = _retention_decay_tables()
    w_in_l, wq, gq, gk = _mla_weights(mla_w_in[0], mla_w_qb[0], mla_q_head_norm[0], mla_k_head_norm[0])
    side = [ret_w_out, ffn_w_in, ffn_w_out, w_in_l, wq, mla_w_kvb[0], mla_w_out]
    qkvg, (ret_w_out_b, ffn_w_in_b, ffn_w_out_b, w_in_l_b, wq_b, wkv_b, mla_w_out_b) = _ret_in_proj(
        x2d, ret_norm[0][None, :], ret_w_in, cos_r, sin_r, seq, side)
    ret = _ret_core(qkvg, dmat, qd, kd, ret_gn[0][:, None, :], batch, seq)
    x2d = _mix_ffn(ret.reshape(batch * seq, -1), ret_w_out_b, x2d, ffn_norm[0][None, :],
                   ffn_w_in_b, ffn_conv_w[0], ffn_conv_b[0][None, :], ffn_w_out_b, seq, layer=0)

    cos_m, sin_m = _rope_tables(seq, MLA_ROPE_DIM // 2)
    pad = jnp.zeros((seq, LANES - MLA_ROPE_DIM), F32)
    cos_t = jnp.concatenate([cos_m, cos_m, pad], axis=1)
    sin_t = jnp.concatenate([-sin_m, sin_m, pad], axis=1)
    q, k, v = _mla_proj(x2d, mla_norm[0][None, :], w_in_l_b, mla_q_norm[0][None, :],
                        mla_kv_norm[0][None, :], wq_b, wkv_b, cos_t, sin_t, gq, gk, batch, seq)
    att = _attention(q, k, v)
    x2d = _mix_ffn(att.reshape(batch * seq, -1), mla_w_out_b, x2d, ffn_norm[1][None, :],
                   ffn_w_in_b, ffn_conv_w[1], ffn_conv_b[1][None, :], ffn_w_out_b, seq, layer=1)
    return x2d.reshape(batch, seq, d)
```

```python
import functools

import jax
import jax.numpy as jnp
from jax import lax
from jax.experimental import pallas as pl
from jax.experimental.pallas import tpu as pltpu

F32 = jnp.float32
BF16 = jnp.bfloat16

CHUNK = 64
RMS_EPS = 1e-6
ROPE_THETA = 10000.0
RET_HEADS = 4
RET_QK_DIM = 256
RET_V_DIM = 512
RET_GAMMA_BASE = -5.0
MLA_HEADS = 8
MLA_Q_RANK = 384
MLA_KV_RANK = 256
MLA_NOPE_DIM = 128
MLA_ROPE_DIM = 64
MLA_V_DIM = 128
MLA_QK_DIM = MLA_NOPE_DIM + MLA_ROPE_DIM
MLA_HEAD_PAD = 256
MLA_Q_COLS = 3 * 128
MASK_VALUE = -1e30
CONV_WIDTH = 3
LOG2_E = 1.4426950408889634

LANES = 128
SUBLANES = 8
VMEM_LIMIT_BYTES = 58 * 1024 * 1024
BF16_ROWS = 16

RET_SUPER = 256
PROJ_TM = 512
RET_HEADS_PER_STEP = 2
FFN_TM = 1024
FFN_CHUNK = 256
MLA_TM = 1024
W_STAGE_BYTES = 384 * 1024
W_STAGE_SLOTS = 4
ATT_TQ = 256
ATT_HEADS_PER_STEP = 4
ATT_AHEAD = 2


def _params(*sem):
    return pltpu.CompilerParams(dimension_semantics=sem, vmem_limit_bytes=VMEM_LIMIT_BYTES)


def _resident(arr):
    return pl.BlockSpec(arr.shape, lambda i: (0,) * arr.ndim, pipeline_mode=pl.Buffered(1))


def _stage_rows(rows, cols):
    limit = W_STAGE_BYTES // (cols * 4)
    fits = [r for r in range(BF16_ROWS, limit + 1, BF16_ROWS) if rows % r == 0]
    return max(fits)


def _weight_scratch(rows, cols):
    return [pltpu.VMEM((rows, cols), BF16), pltpu.VMEM((W_STAGE_SLOTS, _stage_rows(rows, cols), cols), F32),
            pltpu.SemaphoreType.DMA((W_STAGE_SLOTS,))]


def _load_weight(src_ref, dst_ref, stage_ref, sem_ref):
    slots, rows = stage_ref.shape[0], stage_ref.shape[1]
    n_chunks = dst_ref.shape[0] // rows
    lookahead = slots - 1
    assert n_chunks >= lookahead

    def copy(k, slot):
        return pltpu.make_async_copy(src_ref.at[pl.ds(k * rows, rows), :], stage_ref.at[slot], sem_ref.at[slot])

    for k in range(lookahead):
        copy(k, k).start()

    def body(k, carry):
        nxt = k + lookahead

        @pl.when(nxt < n_chunks)
        def _():
            copy(nxt, lax.rem(nxt, slots)).start()

        slot = lax.rem(k, slots)
        copy(k, slot).wait()
        r0 = pl.multiple_of(k * rows, rows)
        dst_ref[pl.ds(r0, rows), :] = stage_ref[slot].astype(BF16)
        return carry

    lax.fori_loop(0, n_chunks, body, 0)


def _rms_scale(x):
    return lax.rsqrt(jnp.mean(x * x, axis=-1, keepdims=True) + RMS_EPS)


def _silu(x):
    half = 0.5 * x
    return half + half * jnp.tanh(half)


def _dot(a, b):
    return jnp.dot(a, b, preferred_element_type=F32)


def _dot_nt(a, b):
    return lax.dot_general(a, b, (((1,), (1,)), ((), ())), preferred_element_type=F32)


def _dot_tn(a, b):
    return lax.dot_general(a, b, (((0,), (0,)), ((), ())), preferred_element_type=F32)


def _ret_in_kernel(*refs, n_side):
    x_ref, gain_ref, w_hbm, cos_ref, sin_ref = refs[:5]
    side_in = refs[5:5 + n_side]
    o_ref = refs[5 + n_side]
    side_out = refs[6 + n_side:6 + 2 * n_side]
    h_ref, w_ref, stage_ref, sem_ref = refs[6 + 2 * n_side:]
    i = pl.program_id(0)
    half = RET_QK_DIM // 2

    @pl.when(i == 0)
    def _():
        _load_weight(w_hbm.at[0], w_ref, stage_ref, sem_ref)

    for src_ref, dst_ref in zip(side_in, side_out):
        dst_ref[...] = src_ref[...].astype(BF16)

    x = x_ref[...]
    h_ref[...] = (x * _rms_scale(x) * gain_ref[...]).astype(BF16)
    cos = cos_ref[...]
    sin = sin_ref[...]
    k_scale = RET_QK_DIM ** -0.5
    cos_k = cos * k_scale
    sin_k = sin * k_scale
    for c in range(w_ref.shape[1] // RET_QK_DIM):
        lo = c * RET_QK_DIM
        acc = _dot(h_ref[...], w_ref[:, lo:lo + RET_QK_DIM])
        if c < 2 * RET_HEADS:
            cs, sn = (cos, sin) if c < RET_HEADS else (cos_k, sin_k)
            x1 = acc[:, :half]
            x2 = acc[:, half:]
            o_ref[:, lo:lo + half] = (x1 * cs - x2 * sn).astype(BF16)
            o_ref[:, lo + half:lo + RET_QK_DIM] = (x2 * cs + x1 * sn).astype(BF16)
        else:
            o_ref[:, lo:lo + RET_QK_DIM] = acc.astype(BF16)


def _cast_plan(arr, n_grid):
    cols = arr.shape[-1]
    rows = arr.size // cols
    steps = max(s for s in range(1, n_grid + 1) if rows % s == 0 and (rows // s) % BF16_ROWS == 0)
    return arr.reshape(rows, cols), steps, rows // steps


def _ret_in_proj(x2d, gain, w, cos, sin, seq, side_weights):
    t, d = x2d.shape
    n = w.shape[-1]
    tm = PROJ_TM
    tps = seq // tm
    n_grid = t // tm
    plans = [_cast_plan(arr, n_grid) for arr in side_weights]
    side_specs = [pl.BlockSpec((rps, view.shape[1]), lambda i, last=steps - 1: (jnp.minimum(i, last), 0))
                  for view, steps, rps in plans]
    kern = functools.partial(_ret_in_kernel, n_side=len(plans))
    outs = pl.pallas_call(
        kern,
        grid=(n_grid,),
        in_specs=[
            pl.BlockSpec((tm, d), lambda i: (i, 0)),
            _resident(gain), pl.BlockSpec(memory_space=pl.ANY),
            pl.BlockSpec((tm, RET_QK_DIM // 2), lambda i: (i % tps, 0)),
            pl.BlockSpec((tm, RET_QK_DIM // 2), lambda i: (i % tps, 0)),
        ] + side_specs,
        out_specs=[pl.BlockSpec((tm, n), lambda i: (i, 0))] + side_specs,
        out_shape=[jax.ShapeDtypeStruct((t, n), BF16)]
        + [jax.ShapeDtypeStruct(view.shape, BF16) for view, _, _ in plans],
        scratch_shapes=[pltpu.VMEM((tm, d), BF16)] + _weight_scratch(d, n),
        compiler_params=_params("arbitrary"),
        name="ret_in_proj",
    )(x2d, gain, w, cos, sin, *[view for view, _, _ in plans])
    return outs[0], [o.reshape(arr.shape) for o, arr in zip(outs[1:], side_weights)]


def _ret_core_kernel(q_ref, k_ref, v_ref, g_ref, dmat_ref, qd_ref, kd_ref, gn_ref, o_ref, state_ref, *, n_steps):
    dk, dv = RET_QK_DIM, RET_V_DIM
    state_ref[...] = jnp.zeros_like(state_ref)

    for sc in range(n_steps):
        rows = slice(sc * RET_SUPER, (sc + 1) * RET_SUPER)
        for hh in range(RET_HEADS_PER_STEP):
            qd = qd_ref[hh]
            kd = kd_ref[hh]
            step_decay = qd[RET_SUPER - 1:RET_SUPER, :]
            q = q_ref[0, rows, hh * dk:(hh + 1) * dk]
            k = k_ref[0, rows, hh * dk:(hh + 1) * dk]
            v = v_ref[0, rows, hh * dv:(hh + 1) * dv]
            scores = _dot_nt(q, k) * dmat_ref[hh]
            inner = _dot(scores.astype(BF16), v)
            state = state_ref[hh]
            q_scaled = (q.astype(F32) * qd).astype(BF16)
            cross = _dot(q_scaled, state.astype(BF16))
            k_scaled = (k.astype(F32) * kd).astype(BF16)
            state_ref[hh] = state * step_decay + _dot_tn(k_scaled, v)
            out = inner + cross
            out = out * _rms_scale(out) * gn_ref[hh]
            g = g_ref[0, rows, hh * dv:(hh + 1) * dv]
            o_ref[0, rows, hh * dv:(hh + 1) * dv] = out.astype(BF16) * _silu(g)


def _ret_core(qkvg, dmat, qd, kd, gn, batch, seq):
    hd, dk, dv = RET_HEADS, RET_QK_DIM, RET_V_DIM
    hps = RET_HEADS_PER_STEP
    groups = hd // hps
    qkvg3 = qkvg.reshape(batch, seq, qkvg.shape[-1])
    k_blk0 = groups
    v_blk0 = 2 * hd * dk // (hps * dv)
    g_blk0 = v_blk0 + groups
    kern = functools.partial(_ret_core_kernel, n_steps=seq // RET_SUPER)
    return pl.pallas_call(
        kern,
        grid=(batch, groups),
        in_specs=[
            pl.BlockSpec((1, seq, hps * dk), lambda b, h: (b, 0, h)),
            pl.BlockSpec((1, seq, hps * dk), lambda b, h: (b, 0, k_blk0 + h)),
            pl.BlockSpec((1, seq, hps * dv), lambda b, h: (b, 0, v_blk0 + h)),
            pl.BlockSpec((1, seq, hps * dv), lambda b, h: (b, 0, g_blk0 + h)),
            pl.BlockSpec((hps, RET_SUPER, RET_SUPER), lambda b, h: (h, 0, 0)),
            pl.BlockSpec((hps, RET_SUPER, 1), lambda b, h: (h, 0, 0)),
            pl.BlockSpec((hps, RET_SUPER, 1), lambda b, h: (h, 0, 0)),
            pl.BlockSpec((hps, 1, dv), lambda b, h: (h, 0, 0)),
        ],
        out_specs=pl.BlockSpec((1, seq, hps * dv), lambda b, h: (b, 0, h)),
        out_shape=jax.ShapeDtypeStruct((batch, seq, hd * dv), BF16),
        scratch_shapes=[pltpu.VMEM((hps, dk, dv), F32)],
        compiler_params=_params("parallel", "parallel"),
        name="ret_core",
    )(qkvg3, qkvg3, qkvg3, qkvg3, dmat, qd, kd, gn)


def _mix_ffn_kernel(a_ref, wo_ref, res_ref, gain_ref, w_in_ref, cw_ref, cb_ref, w_out_ref, o_ref,
                    h_ref, act_ref, carry_ref, *, tiles_per_seq):
    i = pl.program_id(0)
    tm = res_ref.shape[0]
    ffn_dim = w_out_ref.shape[0]

    x1 = res_ref[...] + _dot(a_ref[...], wo_ref[...])
    o_ref[...] = x1
    h_ref[...] = (x1 * _rms_scale(x1) * gain_ref[...]).astype(BF16)

    seq_start = lax.rem(i, tiles_per_seq) == 0
    row = lax.broadcasted_iota(jnp.int32, (tm, 1), 0)
    for c in range(ffn_dim // FFN_CHUNK):
        lo = c * FFN_CHUNK
        a = _dot(h_ref[...], w_in_ref[:, lo:lo + FFN_CHUNK])
        g = _dot(h_ref[...], w_in_ref[:, ffn_dim + lo:ffn_dim + lo + FFN_CHUNK])
        prev = jnp.where(seq_start, 0.0, carry_ref[:, lo:lo + FFN_CHUNK])
        prev1 = prev[SUBLANES - 1:SUBLANES, :]
        prev2 = prev[SUBLANES - 2:SUBLANES - 1, :]
        g1 = jnp.where(row == 0, prev1, pltpu.roll(g, 1, 0))
        g2 = jnp.where(row == 0, prev2, jnp.where(row == 1, prev1, pltpu.roll(g, 2, 0)))
        carry_ref[:, lo:lo + FFN_CHUNK] = g[tm - SUBLANES:tm, :]
        cw = cw_ref[:, lo:lo + FFN_CHUNK]
        gc = g2 * cw[0:1, :] + g1 * cw[1:2, :] + g * cw[2:3, :] + cb_ref[:, lo:lo + FFN_CHUNK]
        act_ref[:, lo:lo + FFN_CHUNK] = (_silu(gc) * a).astype(BF16)

    o_ref[...] += _dot(act_ref[...], w_out_ref[...])


def _resident_layer(arr, layer):
    return pl.BlockSpec((None,) + arr.shape[1:], lambda i: (layer,) + (0,) * (arr.ndim - 1),
                        pipeline_mode=pl.Buffered(1))


def _mix_ffn(a, w_o, res, gain, w_in, conv_w, conv_b, w_out, seq, layer):
    t, d = res.shape
    ka = a.shape[1]
    f = w_out.shape[1]
    tm = FFN_TM
    kern = functools.partial(_mix_ffn_kernel, tiles_per_seq=seq // tm)
    return pl.pallas_call(
        kern,
        grid=(t // tm,),
        in_specs=[
            pl.BlockSpec((tm, ka), lambda i: (i, 0)),
            _resident_layer(w_o, 0),
            pl.BlockSpec((tm, d), lambda i: (i, 0)),
            _resident(gain), _resident_layer(w_in, layer), _resident(conv_w), _resident(conv_b),
            _resident_layer(w_out, layer),
        ],
        out_specs=pl.BlockSpec((tm, d), lambda i: (i, 0)),
        out_shape=jax.ShapeDtypeStruct((t, d), F32),
        scratch_shapes=[pltpu.VMEM((tm, d), BF16), pltpu.VMEM((tm, f), BF16), pltpu.VMEM((SUBLANES, f), F32)],
        compiler_params=_params("arbitrary"),
        name="mix_ffn",
    )(a, w_o, res, gain, w_in, conv_w, conv_b, w_out)


def _mla_proj_kernel(x_ref, gain_ref, w_in_ref, qn_ref, kvn_ref, wq_ref, wkv_ref,
                     c_ref, s_ref, gq_ref, gk_ref, q_ref, k_ref, v_ref):
    x = x_ref[...]
    h = (x * _rms_scale(x) * gain_ref[...]).astype(BF16)
    p = _dot(h, w_in_ref[...])
    c_q = p[:, :MLA_Q_RANK]
    c_kv = p[:, MLA_Q_RANK:MLA_Q_RANK + MLA_KV_RANK]
    k_r = p[:, MLA_Q_RANK + MLA_KV_RANK:MLA_Q_RANK + MLA_KV_RANK + LANES]
    k_s = p[:, MLA_Q_RANK + MLA_KV_RANK + LANES:]
    c_q = (c_q * _rms_scale(c_q) * qn_ref[...]).astype(BF16)
    c_kv = (c_kv * _rms_scale(c_kv) * kvn_ref[...]).astype(BF16)

    cos = c_ref[...]
    sin = s_ref[...]
    gq_nope, gq_rope, gq_swap = gq_ref[0:1, :], gq_ref[1:2, :], gq_ref[2:3, :]
    gk_nope, gk_rope, gk_swap = gk_ref[0:1, :], gk_ref[1:2, :], gk_ref[2:3, :]
    inv_dim = 1.0 / MLA_QK_DIM
    q_scale = MLA_QK_DIM ** -0.5 * LOG2_E

    k_rope_sq = k_r * k_r
    k_rope_rot = k_r * gk_rope * cos + k_s * gk_swap * sin

    def project(pair):
        q2 = _dot(c_q, wq_ref[:, pair * 2 * MLA_Q_COLS:(pair + 1) * 2 * MLA_Q_COLS])
        kv2 = _dot(c_kv, wkv_ref[:, pair * 2 * MLA_HEAD_PAD:(pair + 1) * 2 * MLA_HEAD_PAD])
        return q2, kv2

    def finish(pair, q2, kv2):
        for sub in range(2):
            hh = 2 * pair + sub
            q_n = q2[:, sub * MLA_Q_COLS:sub * MLA_Q_COLS + LANES]
            q_r = q2[:, sub * MLA_Q_COLS + LANES:sub * MLA_Q_COLS + 2 * LANES]
            q_s = q2[:, sub * MLA_Q_COLS + 2 * LANES:(sub + 1) * MLA_Q_COLS]
            ss = jnp.sum(q_n * q_n + q_r * q_r, axis=-1, keepdims=True)
            r = lax.rsqrt(ss * inv_dim + RMS_EPS) * q_scale
            q_ref[0, hh, :, 0:LANES] = (q_n * r * gq_nope).astype(BF16)
            q_ref[0, hh, :, LANES:2 * LANES] = ((q_r * gq_rope * cos + q_s * gq_swap * sin) * r).astype(BF16)

            k_n = kv2[:, sub * MLA_HEAD_PAD:sub * MLA_HEAD_PAD + LANES]
            ssk = jnp.sum(k_n * k_n + k_rope_sq, axis=-1, keepdims=True)
            rk = lax.rsqrt(ssk * inv_dim + RMS_EPS)
            k_ref[0, hh, :, 0:LANES] = (k_n * rk * gk_nope).astype(BF16)
            k_ref[0, hh, :, LANES:2 * LANES] = (k_rope_rot * rk).astype(BF16)
            v_ref[0, hh] = kv2[:, sub * MLA_HEAD_PAD + LANES:(sub + 1) * MLA_HEAD_PAD].astype(BF16)

    n_pairs = MLA_HEADS // 2
    pending = project(0)
    for pair in range(n_pairs):
        upcoming = project(pair + 1) if pair + 1 < n_pairs else None
        finish(pair, *pending)
        pending = upcoming


def _mla_proj(x2d, gain, w_in, qn, kvn, wq, wkv, cos, sin, gq, gk, batch, seq):
    t, d = x2d.shape
    tm = MLA_TM
    tps = seq // tm
    hd = MLA_HEADS

    return pl.pallas_call(
        _mla_proj_kernel,
        grid=(t // tm,),
        in_specs=[
            pl.BlockSpec((tm, d), lambda i: (i, 0)),
            _resident(gain), _resident(w_in), _resident(qn), _resident(kvn), _resident(wq), _resident(wkv),
            pl.BlockSpec((tm, LANES), lambda i: (i % tps, 0)),
            pl.BlockSpec((tm, LANES), lambda i: (i % tps, 0)),
            _resident(gq), _resident(gk),
        ],
        out_specs=[
            pl.BlockSpec((1, hd, tm, MLA_HEAD_PAD), lambda i: (i // tps, 0, i % tps, 0)),
            pl.BlockSpec((1, hd, tm, MLA_HEAD_PAD), lambda i: (i // tps, 0, i % tps, 0)),
            pl.BlockSpec((1, hd, tm, MLA_V_DIM), lambda i: (i // tps, 0, i % tps, 0)),
        ],
        out_shape=[
            jax.ShapeDtypeStruct((batch, hd, seq, MLA_HEAD_PAD), BF16),
            jax.ShapeDtypeStruct((batch, hd, seq, MLA_HEAD_PAD), BF16),
            jax.ShapeDtypeStruct((batch, hd, seq, MLA_V_DIM), BF16),
        ],
        compiler_params=_params("parallel"),
        name="mla_proj",
    )(x2d, gain, w_in, qn, kvn, wq, wkv, cos, sin, gq, gk)


def _attn_kernel(q_ref, k_ref, v_ref, o_ref, vext_ref, *, n_q_blocks):
    tq = ATT_TQ
    row = lax.broadcasted_iota(jnp.int32, (tq, tq), 0) // CHUNK
    col = lax.broadcasted_iota(jnp.int32, (tq, tq), 1) // CHUNK
    visible = col <= row

    vext_ref[:, :, :MLA_V_DIM] = v_ref[0]
    vext_ref[:, :, MLA_V_DIM:] = jnp.ones(vext_ref.shape[:2] + (MLA_V_DIM,), BF16)

    def scores(hh, qi):
        q0 = qi * tq
        q = q_ref[0, hh, q0:q0 + tq, :]
        s_diag = jnp.where(visible, _dot_nt(q, k_ref[0, hh, q0:q0 + tq, :]), MASK_VALUE)
        s_off = _dot_nt(q, k_ref[0, hh, 0:q0, :]) if qi > 0 else None
        return s_diag, s_off

    def finish(hh, qi, s_diag, s_off):
        q0 = qi * tq
        m = jnp.max(s_diag, axis=-1, keepdims=True)
        if qi > 0:
            m = jnp.maximum(m, jnp.max(s_off, axis=-1, keepdims=True))
        p_diag = jnp.exp2(s_diag - m)
        acc = _dot(p_diag.astype(BF16), vext_ref[hh, q0:q0 + tq, :])
        if qi > 0:
            p_off = jnp.exp2(s_off - m)
            acc = acc + _dot(p_off.astype(BF16), vext_ref[hh, 0:q0, :])
        out = acc[:, :MLA_V_DIM] / acc[:, MLA_V_DIM:]
        o_ref[0, q0:q0 + tq, hh * MLA_V_DIM:(hh + 1) * MLA_V_DIM] = out.astype(BF16)

    work = [(hh, qi) for hh in range(ATT_HEADS_PER_STEP) for qi in range(n_q_blocks - 1, -1, -1)]
    queue = [scores(*item) for item in work[:ATT_AHEAD]]
    for pos, item in enumerate(work):
        if pos + ATT_AHEAD < len(work):
            queue.append(scores(*work[pos + ATT_AHEAD]))
        finish(*item, *queue.pop(0))


def _attention(q, k, v):
    batch, hd, seq, _ = q.shape
    hps = ATT_HEADS_PER_STEP
    kern = functools.partial(_attn_kernel, n_q_blocks=seq // ATT_TQ)
    return pl.pallas_call(
        kern,
        grid=(batch, hd // hps),
        in_specs=[
            pl.BlockSpec((1, hps, seq, MLA_HEAD_PAD), lambda b, h: (b, h, 0, 0)),
            pl.BlockSpec((1, hps, seq, MLA_HEAD_PAD), lambda b, h: (b, h, 0, 0)),
            pl.BlockSpec((1, hps, seq, MLA_V_DIM), lambda b, h: (b, h, 0, 0)),
        ],
        out_specs=pl.BlockSpec((1, seq, hps * MLA_V_DIM), lambda b, h: (b, 0, h)),
        out_shape=jax.ShapeDtypeStruct((batch, seq, hd * MLA_V_DIM), BF16),
        scratch_shapes=[pltpu.VMEM((hps, seq, 2 * MLA_V_DIM), BF16)],
        compiler_params=_params("parallel", "parallel"),
        name="mla_attention",
    )(q, k, v)


def _rope_tables(seq, half):
    inv_freq = ROPE_THETA ** (-jnp.arange(half, dtype=F32) / half)
    ang = jnp.arange(seq).astype(F32)[:, None] * inv_freq[None, :]
    return jnp.cos(ang), jnp.sin(ang)


def _retention_decay_tables():
    sc = RET_SUPER
    log_gamma = jnp.log1p(-jnp.exp2(RET_GAMMA_BASE - jnp.arange(RET_HEADS, dtype=F32)))
    idx = jnp.arange(sc, dtype=F32)
    dist = jnp.abs(idx[:, None] - idx[None, :])
    chunk = jnp.arange(sc) // CHUNK
    visible = chunk[None, :] <= chunk[:, None]
    dmat = jnp.where(visible[None], jnp.exp(log_gamma[:, None, None] * dist[None]), 0.0)
    qd = jnp.exp(log_gamma[:, None] * (idx + 1.0))[:, :, None]
    kd = jnp.exp(log_gamma[:, None] * (sc - 1.0 - idx))[:, :, None]
    return dmat, qd, kd


def _swap_halves(w):
    half = w.shape[-1] // 2
    return jnp.concatenate([w[..., half:], w[..., :half]], axis=-1)


def _pad_lanes(w):
    return jnp.pad(w, [(0, 0)] * (w.ndim - 1) + [(0, LANES - w.shape[-1])])


def _mla_weights(w_in, w_qb, q_head_g, k_head_g):
    rq, rkv = MLA_Q_RANK, MLA_KV_RANK
    w_kr = w_in[:, rq + rkv:]
    w_in_l = jnp.concatenate([w_in[:, :rq + rkv], _pad_lanes(w_kr), _pad_lanes(_swap_halves(w_kr))], axis=1)
    w_qb3 = w_qb.reshape(rq, MLA_HEADS, MLA_QK_DIM)
    nope, rope = w_qb3[..., :MLA_NOPE_DIM], w_qb3[..., MLA_NOPE_DIM:]
    wq = jnp.concatenate([nope, _pad_lanes(rope), _pad_lanes(_swap_halves(rope))], axis=-1)
    wq = wq.reshape(rq, MLA_HEADS * MLA_Q_COLS)

    def gains(g):
        rope_g = g[MLA_NOPE_DIM:]
        return jnp.stack([g[:MLA_NOPE_DIM], _pad_lanes(rope_g), _pad_lanes(_swap_halves(rope_g))])

    return w_in_l, wq, gains(q_head_g), gains(k_head_g)


def kernel(x, ret_norm, ret_w_in, ret_gn, ret_w_out, mla_norm, mla_w_in, mla_q_norm, mla_w_qb, mla_kv_norm,
           mla_w_kvb, mla_q_head_norm, mla_k_head_norm, mla_w_out, ffn_norm, ffn_w_in, ffn_conv_w, ffn_conv_b,
           ffn_w_out):
    batch, seq, d = x.shape
    x2d = x.reshape(batch * seq, d)

    cos_r, sin_r = _rope_tables(seq, RET_QK_DIM // 2)
    dmat, qd, kd = _retention_decay_tables()
    w_in_l, wq, gq, gk = _mla_weights(mla_w_in[0], mla_w_qb[0], mla_q_head_norm[0], mla_k_head_norm[0])
    side = [ret_w_out, ffn_w_in, ffn_w_out, w_in_l, wq, mla_w_kvb[0], mla_w_out]
    qkvg, (ret_w_out_b, ffn_w_in_b, ffn_w_out_b, w_in_l_b, wq_b, wkv_b, mla_w_out_b) = _ret_in_proj(
        x2d, ret_norm[0][None, :], ret_w_in, cos_r, sin_r, seq, side)
    ret = _ret_core(qkvg, dmat, qd, kd, ret_gn[0][:, None, :], batch, seq)
    x2d = _mix_ffn(ret.reshape(batch * seq, -1), ret_w_out_b, x2d, ffn_norm[0][None, :],
                   ffn_w_in_b, ffn_conv_w[0], ffn_conv_b[0][None, :], ffn_w_out_b, seq, layer=0)

    cos_m, sin_m = _rope_tables(seq, MLA_ROPE_DIM // 2)
    pad = jnp.zeros((seq, LANES - MLA_ROPE_DIM), F32)
    cos_t = jnp.concatenate([cos_m, cos_m, pad], axis=1)
    sin_t = jnp.concatenate([-sin_m, sin_m, pad], axis=1)
    q, k, v = _mla_proj(x2d, mla_norm[0][None, :], w_in_l_b, mla_q_norm[0][None, :],
                        mla_kv_norm[0][None, :], wq_b, wkv_b, cos_t, sin_t, gq, gk, batch, seq)
    att = _attention(q, k, v)
    x2d = _mix_ffn(att.reshape(batch * seq, -1), mla_w_out_b, x2d, ffn_norm[1][None, :],
                   ffn_w_in_b, ffn_conv_w[1], ffn_conv_b[1][None, :], ffn_w_out_b, seq, layer=1)
    return x2d.reshape(batch, seq, d)
```
